```python
import math
import jax, jax.numpy as jnp
from jax import lax
import numpy as np

D_MODEL = 2048
BATCH = 32
SEQ = 256
DEPTH = 4
DEC_BATCH = 4
DEC_SEQ = 4096
PAST_LEN = 512

GRID_W = 64
N_MIXERS = 4
GROUP_W = D_MODEL // N_MIXERS
N_IN_COLS = 8 * GROUP_W
CHUNK = 128
A_HEAD_DIM = 128
A_HEADS = GROUP_W // A_HEAD_DIM
CONV_W = 31
C_HEADS = 4
C_DK = GROUP_W // (2 * C_HEADS)
C_DV = 2 * C_DK
ROPE_NF = C_DK // 4
ROPE_THETA = 10000.0
D_GROUPS = 4
D_GW = GROUP_W // D_GROUPS
D_FF = 5632
N_MOD = 9
Q_BLOCK = 128
EPS = 1e-6

kernel_name = 'hybrid_diffusion_prefix_step'


def rms_norm(x, g):
    xf = x.astype(jnp.float32)
    y = xf * lax.rsqrt(jnp.mean(xf * xf, axis=-1, keepdims=True) + EPS)
    return (y * g.astype(jnp.float32)).astype(x.dtype)


def layer_norm(x, g, b):
    xf = x.astype(jnp.float32)
    mu = jnp.mean(xf, axis=-1, keepdims=True)
    d = xf - mu
    y = d * lax.rsqrt(jnp.mean(d * d, axis=-1, keepdims=True) + EPS)
    return (y * g.astype(jnp.float32) + b.astype(jnp.float32)).astype(x.dtype)


def swiglu(h, w_up, w_down):
    a, g = jnp.split(h @ w_up, 2, axis=-1)
    return (jax.nn.silu(g) * a) @ w_down


def modulation(cond, w_mod, b_mod):
    return (jax.nn.silu(cond) @ w_mod + b_mod).reshape(cond.shape[0], N_MOD, D_MODEL)


def axial_rope_tables(n_tokens):
    rows = n_tokens // GRID_W
    row = jnp.repeat(jnp.arange(rows, dtype=jnp.float32), GRID_W)
    col = jnp.tile(jnp.arange(GRID_W, dtype=jnp.float32), rows)
    freqs = ROPE_THETA ** (-jnp.arange(ROPE_NF, dtype=jnp.float32) / ROPE_NF)
    ang_r = row[:, None] * freqs
    ang_c = col[:, None] * freqs
    return (jnp.cos(ang_r), jnp.sin(ang_r), jnp.cos(ang_c), jnp.sin(ang_c))


def _rot(xp, cos, sin):
    x1, x2 = xp[..., :ROPE_NF], xp[..., ROPE_NF:]
    return jnp.concatenate([x1 * cos - x2 * sin, x2 * cos + x1 * sin], axis=-1)


def apply_rope(x, tabs):
    cr, sr, cc, sc = [t.reshape(1, t.shape[0], 1, 1, ROPE_NF).astype(x.dtype) for t in tabs]
    half = C_DK // 2
    return jnp.concatenate([_rot(x[..., :half], cr, sr), _rot(x[..., half:], cc, sc)], axis=-1)


def diff_attention(q, k, v, lam):
    bn, lq = q.shape[0], q.shape[1]
    nb = lq // Q_BLOCK
    qb = q.reshape(bn, nb, Q_BLOCK, C_HEADS, 2, C_DK).transpose(1, 0, 2, 3, 4, 5)
    scale = C_DK ** -0.5

    def block(qi):
        s = jnp.einsum('bqhmd,bkhmd->bhmqk', qi, k).astype(jnp.float32) * scale
        p = jax.nn.softmax(s, axis=-1)
        a = p[:, :, 0] - lam * p[:, :, 1]
        return jnp.einsum('bhqk,bkhd->bqhd', a.astype(v.dtype), v)

    o = lax.map(block, qb)
    return o.transpose(1, 0, 2, 3, 4).reshape(bn, lq, C_HEADS, C_DV)


def token_mixers(h, lw, lambda_init, rope, ctx_k, ctx_v):
    bn, L, _ = h.shape
    proj = h @ lw['w_in']
    a_u, a_v, b_a, b_g, c_q, c_k, c_v, d_x = jnp.split(proj, 8, axis=-1)

    u = jax.nn.gelu(a_u)
    vv = rms_norm(jax.nn.gelu(a_v), lw['a_norm_g'])
    vv = vv.reshape(bn, L // CHUNK, CHUNK, A_HEADS, A_HEAD_DIM)
    sp = jnp.einsum('hpq,bnqhc->bnphc', lw['a_ws'], vv) + jnp.swapaxes(lw['a_bs'], 0, 1)[None, None, :, :, None]
    out_a = u * sp.reshape(bn, L, GROUP_W)

    glu = b_a * jax.nn.sigmoid(b_g)
    conv = lax.conv_general_dilated(
        glu, lw['b_conv_w'][:, None, :], window_strides=(1,),
        padding=[(CONV_W // 2, CONV_W // 2)], dimension_numbers=('NWC', 'WIO', 'NWC'),
        feature_group_count=GROUP_W) + lw['b_conv_b']
    out_b = jax.nn.silu(layer_norm(conv, lw['b_ln_g'], lw['b_ln_b'])) @ lw['b_pw']

    q = rms_norm(c_q.reshape(bn, L, C_HEADS, 2, C_DK), lw['c_qnorm_g'])
    k = rms_norm(c_k.reshape(bn, L, C_HEADS, 2, C_DK), lw['c_knorm_g'])
    v = c_v.reshape(bn, L, C_HEADS, C_DV)
    lp = lw['c_lambda'].astype(jnp.float32)
    lam = jnp.exp(jnp.sum(lp[0] * lp[1])) - jnp.exp(jnp.sum(lp[2] * lp[3])) + lambda_init
    if rope is None:
        keys, vals = k, v
        kv = (k.reshape(bn, L, C_HEADS, 2 * C_DK), v)
    else:
        q = apply_rope(q, rope)
        k = apply_rope(k, rope)
        lc = ctx_k.shape[1]
        keys = jnp.concatenate([ctx_k.reshape(bn, lc, C_HEADS, 2, C_DK).astype(k.dtype), k], axis=1)
        vals = jnp.concatenate([ctx_v.astype(v.dtype), v], axis=1)
        kv = None
    o = diff_attention(q, keys, vals, lam)
    o = rms_norm(o, lw['c_subln_g']) * (1.0 - lambda_init)
    out_c = o.reshape(bn, L, GROUP_W)

    dx = d_x.reshape(bn, L, D_GROUPS, D_GW).astype(jnp.float32)
    f = jnp.fft.fft2(dx, axes=(1, 3), norm='ortho').real.astype(h.dtype)
    out_d = f.reshape(bn, L, GROUP_W) @ lw['d_lin']

    y = jnp.concatenate([out_a, out_b, out_c, out_d], axis=-1) @ lw['w_out']
    return y, kv


def trunk_layer(x, mod, lw, lambda_init, rope, ctx_k, ctx_v):
    sh1, sc1, g1, sh2, sc2, g2, sh3, sc3, g3 = [mod[:, None, i, :] for i in range(N_MOD)]
    h = rms_norm(x, lw['norm_g'][0]) * (1 + sc1) + sh1
    x = x + 0.5 * g1 * swiglu(h, lw['w_ff1_in'], lw['w_ff1_down'])
    h = rms_norm(x, lw['norm_g'][1]) * (1 + sc2) + sh2
    y, kv = token_mixers(h, lw, lambda_init, rope, ctx_k, ctx_v)
    x = x + g2 * y
    h = rms_norm(x, lw['norm_g'][2]) * (1 + sc3) + sh3
    x = x + 0.5 * g3 * swiglu(h, lw['w_ff2_in'], lw['w_ff2_down'])
    return x, kv


def setup_inputs(seed: int = 0) -> dict:
    key = jax.random.key(seed)
    ks = jax.random.split(key, 32)
    f32 = jnp.float32

    def nrm(k, shape, s):
        return jax.random.normal(k, shape, f32) * s

    return {
        'x_prompt': nrm(ks[0], (BATCH, SEQ, D_MODEL), 1.0),
        'x_sample': nrm(ks[1], (DEC_BATCH, DEC_SEQ, D_MODEL), 1.0),
        'c': nrm(ks[2], (DEC_BATCH, D_MODEL), 1.0),
        'cache_k': nrm(ks[3], (DEC_BATCH, DEPTH, PAST_LEN, C_HEADS, 2 * C_DK), 1.0),
        'cache_v': nrm(ks[4], (DEC_BATCH, DEPTH, PAST_LEN, C_HEADS, C_DV), 1.0),
        'c_ctx': nrm(ks[5], (D_MODEL,), 1.0),
        'w_mod': nrm(ks[6], (DEPTH, D_MODEL, N_MOD * D_MODEL), 0.5 * D_MODEL ** -0.5),
        'b_mod': nrm(ks[7], (DEPTH, N_MOD * D_MODEL), 0.02),
        'norm_g': 1.0 + nrm(ks[8], (DEPTH, 3, D_MODEL), 0.02),
        'w_ff1_in': nrm(ks[9], (DEPTH, D_MODEL, 2 * D_FF), D_MODEL ** -0.5),
        'w_ff1_down': nrm(ks[10], (DEPTH, D_FF, D_MODEL), D_FF ** -0.5),
        'w_ff2_in': nrm(ks[11], (DEPTH, D_MODEL, 2 * D_FF), D_MODEL ** -0.5),
        'w_ff2_down': nrm(ks[12], (DEPTH, D_FF, D_MODEL), D_FF ** -0.5),
        'w_in': nrm(ks[13], (DEPTH, D_MODEL, N_IN_COLS), D_MODEL ** -0.5),
        'w_out': nrm(ks[14], (DEPTH, N_MIXERS * GROUP_W, D_MODEL), (N_MIXERS * GROUP_W) ** -0.5),
        'a_norm_g': 1.0 + nrm(ks[15], (DEPTH, GROUP_W), 0.02),
        'a_ws': nrm(ks[16], (DEPTH, A_HEADS, CHUNK, CHUNK), CHUNK ** -0.5),
        'a_bs': nrm(ks[17], (DEPTH, A_HEADS, CHUNK), 0.02),
        'b_conv_w': nrm(ks[18], (DEPTH, CONV_W, GROUP_W), CONV_W ** -0.5),
        'b_conv_b': nrm(ks[19], (DEPTH, GROUP_W), 0.02),
        'b_ln_g': 1.0 + nrm(ks[20], (DEPTH, GROUP_W), 0.02),
        'b_ln_b': nrm(ks[21], (DEPTH, GROUP_W), 0.02),
        'b_pw': nrm(ks[22], (DEPTH, GROUP_W, GROUP_W), GROUP_W ** -0.5),
        'c_qnorm_g': 1.0 + nrm(ks[23], (DEPTH, C_DK), 0.02),
        'c_knorm_g': 1.0 + nrm(ks[24], (DEPTH, C_DK), 0.02),
        'c_lambda': nrm(ks[25], (DEPTH, 4, C_DK), 0.1),
        'c_subln_g': 1.0 + nrm(ks[26], (DEPTH, C_DV), 0.02),
        'd_lin': nrm(ks[27], (DEPTH, GROUP_W, GROUP_W), GROUP_W ** -0.5),
    }


def reference(x_prompt, x_sample, c, cache_k, cache_v, c_ctx, w_mod, b_mod, norm_g,
              w_ff1_in, w_ff1_down, w_ff2_in, w_ff2_down, w_in, w_out,
              a_norm_g, a_ws, a_bs, b_conv_w, b_conv_b, b_ln_g, b_ln_b, b_pw,
              c_qnorm_g, c_knorm_g, c_lambda, c_subln_g, d_lin):
    rope = axial_rope_tables(x_sample.shape[1])
    xp, xs = x_prompt, x_sample
    ks_out, vs_out = [], []
    for l in range(DEPTH):
        lw = {
            'norm_g': norm_g[l], 'w_ff1_in': w_ff1_in[l], 'w_ff1_down': w_ff1_down[l],
            'w_ff2_in': w_ff2_in[l], 'w_ff2_down': w_ff2_down[l],
            'w_in': w_in[l], 'w_out': w_out[l],
            'a_norm_g': a_norm_g[l], 'a_ws': a_ws[l], 'a_bs': a_bs[l],
            'b_conv_w': b_conv_w[l], 'b_conv_b': b_conv_b[l], 'b_ln_g': b_ln_g[l],
            'b_ln_b': b_ln_b[l], 'b_pw': b_pw[l],
            'c_qnorm_g': c_qnorm_g[l], 'c_knorm_g': c_knorm_g[l], 'c_lambda': c_lambda[l],
            'c_subln_g': c_subln_g[l], 'd_lin': d_lin[l],
        }
        lambda_init = 0.8 - 0.6 * math.exp(-0.3 * l)
        ctx_mod = modulation(c_ctx[None, :], w_mod[l], b_mod[l])
        xp, (k_l, v_l) = trunk_layer(xp, ctx_mod, lw, lambda_init, None, None, None)
        ks_out.append(k_l)
        vs_out.append(v_l)
        lat_mod = modulation(c, w_mod[l], b_mod[l])
        xs, _ = trunk_layer(xs, lat_mod, lw, lambda_init, rope, cache_k[:, l], cache_v[:, l])
    new_k = jnp.stack(ks_out, axis=1)
    new_v = jnp.stack(vs_out, axis=1)
    return (xp, xs, new_k, new_v)
```

```python
import functools
import math

import jax
import jax.numpy as jnp
from jax import lax
from jax.experimental import pallas as pl
from jax.experimental.pallas import tpu as pltpu

F32 = jnp.float32
BF16 = jnp.bfloat16

EPS = 1e-6
GROUP_W = 512
N_GROUPS = 8
CHUNK = 128
A_HEADS = 4
CONV_W = 31
CONV_HALO = 16
CONV_SUB = 32
C_HEADS = 4
C_DK = 64
C_DV = 128
ROPE_NF = 16
ROPE_THETA = 10000.0
GRID_W = 64
D_GW = 128
N_MOD = 9
COND_PAD = 8
MIB = 1024 * 1024


def _tile(n, target):
    t = target
    while n % t:
        t //= 2
    return t


def _params(sem, vmem_mib):
    return pltpu.CompilerParams(dimension_semantics=sem, vmem_limit_bytes=vmem_mib * MIB)


def _dot(a, b):
    return jnp.dot(a, b, preferred_element_type=F32)


def _mod_kernel(c_ref, w_ref, b_ref, o_ref):
    s = jax.nn.silu(c_ref[...]).astype(BF16)
    o_ref[...] = _dot(s, w_ref[...].astype(BF16)) + b_ref[...]


def _modulation(cond, w_mod, b_mod):
    depth, d, nd = w_mod.shape
    tn = _tile(nd, 1024)
    return pl.pallas_call(
        _mod_kernel,
        grid=(depth, nd // tn),
        in_specs=[
            pl.BlockSpec((COND_PAD, d), lambda l, j: (0, 0)),
            pl.BlockSpec((None, d, tn), lambda l, j: (l, 0, j)),
            pl.BlockSpec((None, 1, tn), lambda l, j: (l, 0, j)),
        ],
        out_specs=pl.BlockSpec((None, COND_PAD, tn), lambda l, j: (l, 0, j)),
        out_shape=jax.ShapeDtypeStruct((depth, COND_PAD, nd), F32),
        compiler_params=_params(("arbitrary", "arbitrary"), 40),
        name="modulation",
    )(cond, w_mod, b_mod.reshape(depth, 1, nd))


def _ada_norm(x, g, shift, scale):
    y = x * lax.rsqrt(jnp.mean(x * x, axis=-1, keepdims=True) + EPS) * g
    return y * (1.0 + scale) + shift


def _ffn_kernel(x_ref, mod_ref, g_ref, wa_ref, wg_ref, wd_ref, o_ref, h_ref, *, s, nf):
    f = pl.program_id(1)

    @pl.when(f == 0)
    def _():
        h = _ada_norm(x_ref[...], g_ref[...], mod_ref[s:s + 1, :], mod_ref[s + 1:s + 2, :])
        h_ref[...] = h.astype(BF16)

    h = h_ref[...]
    a = _dot(h, wa_ref[...])
    g = _dot(h, wg_ref[...])
    act = (jax.nn.silu(g) * a).astype(BF16)
    d = _dot(act, wd_ref[...])

    @pl.when(f == 0)
    def _():
        o_ref[...] = d

    @pl.when(f > 0)
    def _():
        o_ref[...] += d

    @pl.when(f == nf - 1)
    def _():
        o_ref[...] = x_ref[...] + (0.5 * mod_ref[s + 2:s + 3, :]) * o_ref[...]


def _ffn(x, modl, g, w_up, w_down, s, cond_of):
    n, d = x.shape
    ff = w_down.shape[0]
    tm = cond_of.tile(512)
    tf = _tile(ff, 512)
    nf = ff // tf
    return pl.pallas_call(
        functools.partial(_ffn_kernel, s=s, nf=nf),
        grid=(n // tm, nf),
        in_specs=[
            pl.BlockSpec((tm, d), lambda i, f: (i, 0)),
            pl.BlockSpec((None, N_MOD, d), lambda i, f: (cond_of(i, tm), 0, 0)),
            pl.BlockSpec((1, d), lambda i, f: (0, 0)),
            pl.BlockSpec((d, tf), lambda i, f: (0, f)),
            pl.BlockSpec((d, tf), lambda i, f: (0, f + nf)),
            pl.BlockSpec((tf, d), lambda i, f: (f, 0)),
        ],
        out_specs=pl.BlockSpec((tm, d), lambda i, f: (i, 0)),
        out_shape=jax.ShapeDtypeStruct((n, d), F32),
        scratch_shapes=[pltpu.VMEM((tm, d), BF16)],
        compiler_params=_params(("arbitrary", "arbitrary"), 56),
        name="ffn",
    )(x, modl, g, w_up, w_up, w_down)


def _inproj_kernel(x_ref, mod_ref, g_ref, w_ref, o_ref, h_ref):
    @pl.when(pl.program_id(1) == 0)
    def _():
        h = _ada_norm(x_ref[...], g_ref[...], mod_ref[3:4, :], mod_ref[4:5, :])
        h_ref[...] = h.astype(BF16)

    o_ref[...] = _dot(h_ref[...], w_ref[...])


def _inproj(x, modl, g, w_in, cond_of):
    n, d = x.shape
    nc = w_in.shape[1]
    tm = cond_of.tile(1024)
    tn = _tile(nc, 1024)
    return pl.pallas_call(
        _inproj_kernel,
        grid=(n // tm, nc // tn),
        in_specs=[
            pl.BlockSpec((tm, d), lambda i, j: (i, 0)),
            pl.BlockSpec((None, N_MOD, d), lambda i, j: (cond_of(i, tm), 0, 0)),
            pl.BlockSpec((1, d), lambda i, j: (0, 0)),
            pl.BlockSpec((d, tn), lambda i, j: (0, j)),
        ],
        out_specs=pl.BlockSpec((tm, tn), lambda i, j: (i, j)),
        out_shape=jax.ShapeDtypeStruct((n, nc), F32),
        scratch_shapes=[pltpu.VMEM((tm, d), BF16)],
        compiler_params=_params(("arbitrary", "arbitrary"), 48),
        name="in_proj",
    )(x, modl, g, w_in)


def _mix_a_kernel(u_ref, v_ref, g_ref, ws_ref, bias_ref, o_ref, vv_ref, *, nchunk):
    v = jax.nn.gelu(v_ref[...])
    vv = v * lax.rsqrt(jnp.mean(v * v, axis=-1, keepdims=True) + EPS) * g_ref[...]
    vv_ref[...] = vv.astype(BF16)
    for n in range(nchunk):
        rows = slice(n * CHUNK, (n + 1) * CHUNK)
        for h in range(A_HEADS):
            cols = slice(h * CHUNK, (h + 1) * CHUNK)
            sp = _dot(ws_ref[h], vv_ref[rows, cols]) + bias_ref[:, cols]
            o_ref[rows, cols] = (jax.nn.gelu(u_ref[rows, cols]) * sp).astype(BF16)


def _mix_a(proj, a_norm_g, a_ws, bias):
    n = proj.shape[0]
    t = 512
    return pl.pallas_call(
        functools.partial(_mix_a_kernel, nchunk=t // CHUNK),
        grid=(n // t,),
        in_specs=[
            pl.BlockSpec((t, GROUP_W), lambda i: (i, 0)),
            pl.BlockSpec((t, GROUP_W), lambda i: (i, 1)),
            pl.BlockSpec((1, GROUP_W), lambda i: (0, 0)),
            pl.BlockSpec((A_HEADS, CHUNK, CHUNK), lambda i: (0, 0, 0)),
            pl.BlockSpec((CHUNK, GROUP_W), lambda i: (0, 0)),
        ],
        out_specs=pl.BlockSpec((t, GROUP_W), lambda i: (i, 0)),
        out_shape=jax.ShapeDtypeStruct((n, GROUP_W), BF16),
        scratch_shapes=[pltpu.VMEM((t, GROUP_W), BF16)],
        compiler_params=_params(("arbitrary",), 32),
        name="mix_a",
    )(proj, proj, a_norm_g, a_ws, bias)


def _mix_b_kernel(ap_ref, ac_ref, an_ref, gp_ref, gc_ref, gn_ref, w_ref, cb_ref, lg_ref, lb_ref,
                  pw_ref, o_ref, pad_ref, conv_ref, *, t, n_prompt, seq_p, seq_s):
    row0 = pl.program_id(0) * t
    in_prompt = row0 < n_prompt
    off = jnp.where(in_prompt, row0, row0 - n_prompt)
    seq = jnp.where(in_prompt, seq_p, seq_s)
    has_prev = lax.rem(off, seq) != 0
    has_next = lax.rem(off + t, seq) != 0

    def glu(a_ref, g_ref):
        return a_ref[...] * jax.nn.sigmoid(g_ref[...])

    pad_ref[0:CONV_HALO, :] = jnp.where(has_prev, glu(ap_ref, gp_ref), 0.0)
    pad_ref[CONV_HALO:CONV_HALO + t, :] = glu(ac_ref, gc_ref)
    pad_ref[CONV_HALO + t:2 * CONV_HALO + t, :] = jnp.where(has_next, glu(an_ref, gn_ref), 0.0)

    base = CONV_HALO - CONV_W // 2
    for r in range(t // CONV_SUB):
        acc = jnp.broadcast_to(cb_ref[...], (CONV_SUB, GROUP_W))
        for j in range(CONV_W):
            start = r * CONV_SUB + base + j
            acc = acc + pad_ref[start:start + CONV_SUB, :] * w_ref[j:j + 1, :]
        conv_ref[r * CONV_SUB:(r + 1) * CONV_SUB, :] = acc

    c = conv_ref[...]
    mu = jnp.mean(c, axis=-1, keepdims=True)
    dlt = c - mu
    y = dlt * lax.rsqrt(jnp.mean(dlt * dlt, axis=-1, keepdims=True) + EPS) * lg_ref[...] + lb_ref[...]
    o_ref[...] = _dot(jax.nn.silu(y).astype(BF16), pw_ref[...]).astype(BF16)


def _mix_b(proj, conv_w, conv_b, ln_g, ln_b, pw, n_prompt, seq_p, seq_s):
    n = proj.shape[0]
    t = _tile(math.gcd(seq_p, seq_s), 256)
    hb = t // CONV_HALO
    nhb = n // CONV_HALO
    cur = lambda c: pl.BlockSpec((t, GROUP_W), lambda i: (i, c))
    prev = lambda c: pl.BlockSpec((CONV_HALO, GROUP_W), lambda i: (jnp.maximum(i * hb - 1, 0), c))
    nxt = lambda c: pl.BlockSpec((CONV_HALO, GROUP_W), lambda i: (jnp.minimum((i + 1) * hb, nhb - 1), c))
    vec = pl.BlockSpec((1, GROUP_W), lambda i: (0, 0))
    return pl.pallas_call(
        functools.partial(_mix_b_kernel, t=t, n_prompt=n_prompt, seq_p=seq_p, seq_s=seq_s),
        grid=(n // t,),
        in_specs=[prev(2), cur(2), nxt(2), prev(3), cur(3), nxt(3),
                  pl.BlockSpec((CONV_W, GROUP_W), lambda i: (0, 0)), vec, vec, vec,
                  pl.BlockSpec((GROUP_W, GROUP_W), lambda i: (0, 0))],
        out_specs=pl.BlockSpec((t, GROUP_W), lambda i: (i, 0)),
        out_shape=jax.ShapeDtypeStruct((n, GROUP_W), BF16),
        scratch_shapes=[pltpu.VMEM((t + 2 * CONV_HALO, GROUP_W), F32), pltpu.VMEM((t, GROUP_W), F32)],
        compiler_params=_params(("arbitrary",), 32),
        name="mix_b",
    )(proj, proj, proj, proj, proj, proj, conv_w, conv_b, ln_g, ln_b, pw)


def _group_mean_sq(x, gm_ref):
    s = x * x
    s_hi = s.astype(BF16)
    s_lo = (s - s_hi.astype(F32)).astype(BF16)
    return _dot(s_hi, gm_ref[...]) + _dot(s_lo, gm_ref[...])


def _swap16(x):
    n = x.shape[-1]
    lane = lax.broadcasted_iota(jnp.int32, x.shape, x.ndim - 1)
    first = lax.rem(lane, 32) < 16
    return jnp.where(first, pltpu.roll(x, n - 16, x.ndim - 1), pltpu.roll(x, 16, x.ndim - 1))


def _prep_prompt_kernel(q_ref, k_ref, v_ref, gq_ref, gk_ref, gm_ref, qo_ref, ko_ref, vo_ref, nk_ref, nv_ref):
    q = q_ref[...]
    k = k_ref[...]
    v = v_ref[...]
    qn = q * lax.rsqrt(_group_mean_sq(q, gm_ref) + EPS) * gq_ref[...]
    kn = k * lax.rsqrt(_group_mean_sq(k, gm_ref) + EPS) * gk_ref[...]
    nk_ref[...] = kn
    nv_ref[...] = v
    qo_ref[...] = (qn * (C_DK ** -0.5)).astype(BF16)
    ko_ref[...] = kn.astype(BF16)
    vo_ref[...] = v.astype(BF16)


def _prep_latent_kernel(q_ref, k_ref, v_ref, gq_ref, gk_ref, gm_ref, cos_ref, sin_ref, qo_ref, ko_ref, vo_ref):
    q = q_ref[...]
    k = k_ref[...]
    qn = q * lax.rsqrt(_group_mean_sq(q, gm_ref) + EPS) * gq_ref[...]
    kn = k * lax.rsqrt(_group_mean_sq(k, gm_ref) + EPS) * gk_ref[...]
    cos = cos_ref[...]
    sin = sin_ref[...]
    qr = qn * cos + _swap16(qn) * sin
    kr = kn * cos + _swap16(kn) * sin
    qo_ref[...] = (qr * (C_DK ** -0.5)).astype(BF16)
    ko_ref[...] = kr.astype(BF16)
    vo_ref[...] = v_ref[...].astype(BF16)


def _prep(proj, gq, gk, gm, row0, nrows, rope):
    t = _tile(nrows, 512)
    rb = row0 // t
    col = lambda c: pl.BlockSpec((t, GROUP_W), lambda i: (rb + i, c))
    vec = pl.BlockSpec((1, GROUP_W), lambda i: (0, 0))
    out = pl.BlockSpec((t, GROUP_W), lambda i: (i, 0))
    in_specs = [col(4), col(5), col(6), vec, vec, pl.BlockSpec((GROUP_W, GROUP_W), lambda i: (0, 0))]
    args = [proj, proj, proj, gq, gk, gm]
    shp = lambda dt: jax.ShapeDtypeStruct((nrows, GROUP_W), dt)
    if rope is None:
        body, out_specs, out_shape = _prep_prompt_kernel, [out] * 5, [shp(BF16)] * 3 + [shp(F32)] * 2
    else:
        cos, sin = rope
        npos = cos.shape[0] // t
        tab = pl.BlockSpec((t, GROUP_W), lambda i: (lax.rem(i, npos), 0))
        in_specs += [tab, tab]
        args += [cos, sin]
        body, out_specs, out_shape = _prep_latent_kernel, [out] * 3, [shp(BF16)] * 3
    return pl.pallas_call(
        body, grid=(nrows // t,), in_specs=in_specs, out_specs=out_specs, out_shape=out_shape,
        compiler_params=_params(("arbitrary",), 32),
        name="qkv_prep_prompt" if rope is None else "qkv_prep_latent",
    )(*args)


def _attn_kernel(*refs, lam_init, aliased):
    lam_ref, q_ref, k_ref, v_ref, sg_ref = refs[:5]
    o_ref = refs[-1]
    lp = lam_ref[...]
    lam = (jnp.exp(jnp.sum(lp[0:1, :] * lp[1:2, :], axis=-1, keepdims=True))
           - jnp.exp(jnp.sum(lp[2:3, :] * lp[3:4, :], axis=-1, keepdims=True)) + lam_init)
    tq = q_ref.shape[0]
    lane = lax.broadcasted_iota(jnp.int32, (tq, C_DV), 1)
    first = lane < C_DK
    dims = (((1,), (1,)), ((), ()))

    def softmax(s):
        e = jnp.exp(s - jnp.max(s, axis=-1, keepdims=True))
        return e * (1.0 / jnp.sum(e, axis=-1, keepdims=True))

    for h in range(C_HEADS):
        cols = slice(h * C_DV, (h + 1) * C_DV)
        qh = q_ref[:, cols]
        kh = k_ref[:, cols]
        zero = jnp.zeros_like(qh)
        s1 = lax.dot_general(jnp.where(first, qh, zero), kh, dims, preferred_element_type=F32)
        s2 = lax.dot_general(jnp.where(first, zero, qh), kh, dims, preferred_element_type=F32)
        a = (softmax(s1) - lam * softmax(s2)).astype(BF16)
        o = _dot(a, v_ref[:, cols])
        o = o * lax.rsqrt(jnp.mean(o * o, axis=-1, keepdims=True) + EPS) * sg_ref[...]
        o_ref[:, cols] = (o * (1.0 - lam_init)).astype(BF16)


def _attention(c_lambda, q, k, v, subln_g, lam_init, n_total, row0, prev_out):
    nb, lk, _ = k.shape
    lq = q.shape[0] // nb
    tq = _tile(lq, 256)
    nq = lq // tq
    rb = row0 // tq
    in_specs = [
        pl.BlockSpec((4, C_DK), lambda b, t: (0, 0)),
        pl.BlockSpec((tq, GROUP_W), lambda b, t: (b * nq + t, 0)),
        pl.BlockSpec((None, lk, GROUP_W), lambda b, t: (b, 0, 0)),
        pl.BlockSpec((None, lk, GROUP_W), lambda b, t: (b, 0, 0)),
        pl.BlockSpec((1, C_DV), lambda b, t: (0, 0)),
    ]
    args = [c_lambda, q, k, v, subln_g]
    aliases = {}
    if prev_out is not None:
        in_specs.append(pl.BlockSpec(memory_space=pl.ANY))
        args.append(prev_out)
        aliases = {5: 0}
    return pl.pallas_call(
        functools.partial(_attn_kernel, lam_init=lam_init, aliased=prev_out is not None),
        grid=(nb, nq),
        in_specs=in_specs,
        out_specs=pl.BlockSpec((tq, GROUP_W), lambda b, t: (rb + b * nq + t, 0)),
        out_shape=jax.ShapeDtypeStruct((n_total, GROUP_W), BF16),
        input_output_aliases=aliases,
        compiler_params=_params(("arbitrary", "arbitrary"), 56),
        name="diff_attn",
    )(*args)


def _dft_kernel(*refs, scale):
    dx_ref, wc_ref, c_ref, s_ref, dl_ref = refs[:5]
    o_ref, pq_ref = refs[-2], refs[-1]

    @pl.when(pl.program_id(1) == 0)
    def _():
        pq_ref[...] = _dot(dx_ref[...].astype(BF16), wc_ref[...]).astype(BF16)

    y = _dot(c_ref[...], pq_ref[:, 0:GROUP_W]) - _dot(s_ref[...], pq_ref[:, GROUP_W:2 * GROUP_W])
    o_ref[...] = _dot((y * scale).astype(BF16), dl_ref[...]).astype(BF16)


def _fourier(proj, wc, cl, sl, d_lin, nb, seq, n_total, row0, prev_out):
    tr = _tile(seq, 512)
    nr = seq // tr
    sb = row0 // seq
    rb = row0 // tr
    in_specs = [
        pl.BlockSpec((seq, GROUP_W), lambda b, r: (sb + b, N_GROUPS - 1)),
        pl.BlockSpec((GROUP_W, 2 * GROUP_W), lambda b, r: (0, 0)),
        pl.BlockSpec((tr, seq), lambda b, r: (r, 0)),
        pl.BlockSpec((tr, seq), lambda b, r: (r, 0)),
        pl.BlockSpec((GROUP_W, GROUP_W), lambda b, r: (0, 0)),
    ]
    args = [proj, wc, cl, sl, d_lin]
    aliases = {}
    if prev_out is not None:
        in_specs.append(pl.BlockSpec(memory_space=pl.ANY))
        args.append(prev_out)
        aliases = {5: 0}
    return pl.pallas_call(
        functools.partial(_dft_kernel, scale=1.0 / math.sqrt(seq * D_GW)),
        grid=(nb, nr),
        in_specs=in_specs,
        out_specs=pl.BlockSpec((tr, GROUP_W), lambda b, r: (rb + b * nr + r, 0)),
        out_shape=jax.ShapeDtypeStruct((n_total, GROUP_W), BF16),
        scratch_shapes=[pltpu.VMEM((seq, 2 * GROUP_W), BF16)],
        input_output_aliases=aliases,
        compiler_params=_params(("arbitrary", "arbitrary"), 56),
        name="fourier_mix",
    )(*args)


def _outproj_kernel(x_ref, mod_ref, a_ref, b_ref, c_ref, d_ref, w_ref, o_ref):
    y = _dot(a_ref[...], w_ref[0:GROUP_W, :])
    y += _dot(b_ref[...], w_ref[GROUP_W:2 * GROUP_W, :])
    y += _dot(c_ref[...], w_ref[2 * GROUP_W:3 * GROUP_W, :])
    y += _dot(d_ref[...], w_ref[3 * GROUP_W:4 * GROUP_W, :])
    o_ref[...] = x_ref[...] + mod_ref[5:6, :] * y


def _outproj(x, modl, oa, ob, oc, od, w_out, cond_of):
    n, d = x.shape
    tm = cond_of.tile(512)
    mix = pl.BlockSpec((tm, GROUP_W), lambda i: (i, 0))
    return pl.pallas_call(
        _outproj_kernel,
        grid=(n // tm,),
        in_specs=[
            pl.BlockSpec((tm, d), lambda i: (i, 0)),
            pl.BlockSpec((None, N_MOD, d), lambda i: (cond_of(i, tm), 0, 0)),
            mix, mix, mix, mix,
            pl.BlockSpec((4 * GROUP_W, d), lambda i: (0, 0)),
        ],
        out_specs=pl.BlockSpec((tm, d), lambda i: (i, 0)),
        out_shape=jax.ShapeDtypeStruct((n, d), F32),
        compiler_params=_params(("arbitrary",), 48),
        name="out_proj",
    )(x, modl, oa, ob, oc, od, w_out)


def _dft_tables(n):
    j = lax.broadcasted_iota(jnp.int32, (n, n), 0)
    k = lax.broadcasted_iota(jnp.int32, (n, n), 1)
    ang = (2.0 * math.pi / n) * lax.rem(j * k, n).astype(F32)
    return jnp.cos(ang), jnp.sin(ang)


def _channel_dft_table():
    c, s = _dft_tables(D_GW)
    eye = jnp.eye(GROUP_W // D_GW, dtype=F32)
    return jnp.concatenate([jnp.kron(eye, c), jnp.kron(eye, s)], axis=1).astype(BF16)


def _rope_tables(n_tokens):
    rows = n_tokens // GRID_W
    row = jnp.repeat(jnp.arange(rows, dtype=F32), GRID_W)
    col = jnp.tile(jnp.arange(GRID_W, dtype=F32), rows)
    freqs = ROPE_THETA ** (-jnp.arange(ROPE_NF, dtype=F32) / ROPE_NF)
    ang_r = row[:, None] * freqs
    ang_c = col[:, None] * freqs

    def lanes(fr, fc, sign):
        grp = jnp.concatenate([sign * fr, fr, sign * fc, fc], axis=-1)
        return jnp.tile(grp, (1, GROUP_W // C_DK))

    cos = lanes(jnp.cos(ang_r), jnp.cos(ang_c), 1.0)
    sin = lanes(jnp.sin(ang_r), jnp.sin(ang_c), -1.0)
    return cos, sin


class _CondOf:
    def __init__(self, n_prompt, seq_s):
        self.n_prompt = n_prompt
        self.seq_s = seq_s

    def tile(self, target):
        return _tile(math.gcd(self.n_prompt, self.seq_s), target)

    def __call__(self, i, tm):
        r = i * tm
        return jnp.where(r < self.n_prompt, 0, 1 + (r - self.n_prompt) // self.seq_s)


def kernel(x_prompt, x_sample, c, cache_k, cache_v, c_ctx, w_mod, b_mod, norm_g, w_ff1_in, w_ff1_down, w_ff2_in, w_ff2_down, w_in, w_out, a_norm_g, a_ws, a_bs, b_conv_w, b_conv_b, b_ln_g, b_ln_b, b_pw, c_qnorm_g, c_knorm_g, c_lambda, c_subln_g, d_lin):
    nbp, seq_p, d = x_prompt.shape
    nbs, seq_s, _ = x_sample.shape
    depth = w_mod.shape[0]
    past = cache_k.shape[2]
    n_p, n_s = nbp * seq_p, nbs * seq_s
    n = n_p + n_s
    assert n_p % seq_s == 0 and seq_p % CHUNK == 0 and seq_s % CHUNK == 0
    assert 1 + nbs <= COND_PAD
    cond_of = _CondOf(n_p, seq_s)

    x = jnp.concatenate([x_prompt.reshape(n_p, d), x_sample.reshape(n_s, d)], axis=0)
    cond = jnp.zeros((COND_PAD, d), F32).at[0].set(c_ctx).at[1:1 + nbs].set(c)
    mods = _modulation(cond, w_mod, b_mod).reshape(depth, COND_PAD, N_MOD, d)

    wc = _channel_dft_table()
    tabs_p = [t.astype(BF16) for t in _dft_tables(seq_p)]
    tabs_s = [t.astype(BF16) for t in _dft_tables(seq_s)]
    rope = _rope_tables(seq_s)
    gidx = jnp.arange(GROUP_W) // C_DK
    gm = jnp.where(gidx[:, None] == gidx[None, :], 1.0 / C_DK, 0.0).astype(BF16)

    ks_out, vs_out = [], []
    for l in range(depth):
        lam_init = 0.8 - 0.6 * math.exp(-0.3 * l)
        modl = mods[l]
        ng = norm_g[l]
        x = _ffn(x, modl, ng[0:1], w_ff1_in[l].astype(BF16), w_ff1_down[l].astype(BF16), 0, cond_of)

        proj = _inproj(x, modl, ng[1:2], w_in[l].astype(BF16), cond_of)

        bias_a = jnp.repeat(a_bs[l].T, CHUNK, axis=1)
        out_a = _mix_a(proj, a_norm_g[l][None], a_ws[l].astype(BF16), bias_a)

        out_b = _mix_b(proj, b_conv_w[l], b_conv_b[l][None], b_ln_g[l][None], b_ln_b[l][None],
                       b_pw[l].astype(BF16), n_p, seq_p, seq_s)

        gq = jnp.tile(c_qnorm_g[l], GROUP_W // C_DK)[None]
        gk = jnp.tile(c_knorm_g[l], GROUP_W // C_DK)[None]
        sg = c_subln_g[l][None]
        qp, kp, vp, nk, nv = _prep(proj, gq, gk, gm, 0, n_p, None)
        ks_out.append(nk.reshape(nbp, seq_p, C_HEADS, 2 * C_DK))
        vs_out.append(nv.reshape(nbp, seq_p, C_HEADS, C_DV))
        qs, ksl, vsl = _prep(proj, gq, gk, gm, n_p, n_s, rope)
        k_all = jnp.concatenate([cache_k[:, l].reshape(nbs, past, GROUP_W).astype(BF16),
                                 ksl.reshape(nbs, seq_s, GROUP_W)], axis=1)
        v_all = jnp.concatenate([cache_v[:, l].reshape(nbs, past, GROUP_W).astype(BF16),
                                 vsl.reshape(nbs, seq_s, GROUP_W)], axis=1)
        out_c = _attention(c_lambda[l], qp, kp.reshape(nbp, seq_p, GROUP_W), vp.reshape(nbp, seq_p, GROUP_W),
                           sg, lam_init, n, 0, None)
        out_c = _attention(c_lambda[l], qs, k_all, v_all, sg, lam_init, n, n_p, out_c)

        dl = d_lin[l].astype(BF16)
        out_d = _fourier(proj, wc, tabs_p[0], tabs_p[1], dl, nbp, seq_p, n, 0, None)
        out_d = _fourier(proj, wc, tabs_s[0], tabs_s[1], dl, nbs, seq_s, n, n_p, out_d)

        x = _outproj(x, modl, out_a, out_b, out_c, out_d, w_out[l].astype(BF16), cond_of)
        x = _ffn(x, modl, ng[2:3], w_ff2_in[l].astype(BF16), w_ff2_down[l].astype(BF16), 6, cond_of)

    new_k = jnp.stack(ks_out, axis=1)
    new_v = jnp.stack(vs_out, axis=1)
    return (x[:n_p].reshape(nbp, seq_p, d), x[n_p:].reshape(nbs, seq_s, d), new_k, new_v)
```

```python
import functools
import math

import jax
import jax.numpy as jnp
from jax import lax
from jax.experimental import pallas as pl
from jax.experimental.pallas import tpu as pltpu

F32 = jnp.float32
BF16 = jnp.bfloat16

EPS = 1e-6
GROUP_W = 512
N_GROUPS = 8
CHUNK = 128
A_HEADS = 4
CONV_W = 31
CONV_HALO = 16
CONV_SUB = 32
SUBLANES = 8
C_HEADS = 4
C_DK = 64
C_DV = 128
Q_SCALE = C_DK ** -0.5 * math.log2(math.e)
ROPE_NF = 16
ROPE_THETA = 10000.0
GRID_W = 64
D_GW = 128
N_MOD = 9
COND_PAD = 8
MIB = 1024 * 1024
FFN_TM = 1024
FFN_NSPLIT = 4
FFN_VMEM_MIB = 60
NORM_ROWS = 16
NORM_UNROLL = 8


def _tile(n, target):
    t = target
    while n % t:
        t //= 2
    return t


def _params(sem, vmem_mib):
    return pltpu.CompilerParams(dimension_semantics=sem, vmem_limit_bytes=vmem_mib * MIB)


def _dot(a, b):
    return jnp.dot(a, b, preferred_element_type=F32)


def _mod_kernel(c_ref, w_ref, b_ref, o_ref):
    s = jax.nn.silu(c_ref[...]).astype(BF16)
    o_ref[...] = _dot(s, w_ref[...].astype(BF16)) + b_ref[...]


def _modulation(cond, w_mod, b_mod):
    depth, d, nd = w_mod.shape
    tn = _tile(nd, 1024)
    return pl.pallas_call(
        _mod_kernel,
        grid=(depth, nd // tn),
        in_specs=[
            pl.BlockSpec((COND_PAD, d), lambda l, j: (0, 0)),
            pl.BlockSpec((None, d, tn), lambda l, j: (l, 0, j)),
            pl.BlockSpec((None, 1, tn), lambda l, j: (l, 0, j)),
        ],
        out_specs=pl.BlockSpec((None, COND_PAD, tn), lambda l, j: (l, 0, j)),
        out_shape=jax.ShapeDtypeStruct((depth, COND_PAD, nd), F32),
        compiler_params=_params(("arbitrary", "arbitrary"), 40),
        name="modulation",
    )(cond, w_mod, b_mod.reshape(depth, 1, nd))


def _ada_norm_rows(x_ref, h_ref, g, shift, scale, copy_ref=None):
    gain = g * (1.0 + scale)

    def body(c, carry):
        rows = pl.ds(pl.multiple_of(c * NORM_ROWS, NORM_ROWS), NORM_ROWS)
        x = x_ref[rows, :]
        y = x * lax.rsqrt(jnp.mean(x * x, axis=-1, keepdims=True) + EPS)
        h_ref[rows, :] = (y * gain + shift).astype(BF16)
        if copy_ref is not None:
            copy_ref[rows, :] = x
        return carry

    lax.fori_loop(0, x_ref.shape[0] // NORM_ROWS, body, 0, unroll=NORM_UNROLL)


def _ffn_kernel(*refs, s, n_split):
    x_ref, mod_ref, g_ref, wa_ref, wg_ref, wd_ref = refs[:6]
    o_ref, h_ref = refs[-2:]

    @pl.when(pl.program_id(1) == 0)
    def _():
        _ada_norm_rows(x_ref, h_ref, g_ref[...], mod_ref[s:s + 1, :], mod_ref[s + 1:s + 2, :], copy_ref=o_ref)

    h = h_ref[...]
    a = _dot(h, wa_ref[...])
    g = _dot(h, wg_ref[...])
    act = (jax.nn.silu(g) * a).astype(BF16)
    gate = 0.5 * mod_ref[s + 2:s + 3, :]
    cw = o_ref.shape[1] // n_split
    for c in range(n_split):
        cols = slice(c * cw, (c + 1) * cw)
        o_ref[:, cols] += gate[:, cols] * _dot(act, wd_ref[:, cols])


def _ffn(x, modl, g, w_up, w_down, s, cond_of, *, in_row0=0, nrows=None, tok_row0=0,
         out_total=None, out_row0=0, prev_out=None):
    d = x.shape[1]
    nrows = x.shape[0] if nrows is None else nrows
    out_total = nrows if out_total is None else out_total
    ff = w_down.shape[0]
    tm = cond_of.tile(FFN_TM)
    tf = _tile(ff, 512)
    nf = ff // tf
    ib, tb, ob = in_row0 // tm, tok_row0 // tm, out_row0 // tm
    in_specs = [
        pl.BlockSpec((tm, d), lambda i, f: (ib + i, 0)),
        pl.BlockSpec((None, N_MOD, d), lambda i, f: (cond_of(tb + i, tm), 0, 0)),
        pl.BlockSpec((1, d), lambda i, f: (0, 0)),
        pl.BlockSpec((d, tf), lambda i, f: (0, f)),
        pl.BlockSpec((d, tf), lambda i, f: (0, f + nf)),
        pl.BlockSpec((tf, d), lambda i, f: (f, 0)),
    ]
    args = [x, modl, g, w_up, w_up, w_down]
    aliases = {}
    if prev_out is not None:
        in_specs.append(pl.BlockSpec(memory_space=pl.ANY))
        args.append(prev_out)
        aliases = {6: 0}
    return pl.pallas_call(
        functools.partial(_ffn_kernel, s=s, n_split=FFN_NSPLIT),
        grid=(nrows // tm, nf),
        in_specs=in_specs,
        out_specs=pl.BlockSpec((tm, d), lambda i, f: (ob + i, 0)),
        out_shape=jax.ShapeDtypeStruct((out_total, d), F32),
        scratch_shapes=[pltpu.VMEM((tm, d), BF16)],
        input_output_aliases=aliases,
        compiler_params=_params(("arbitrary", "arbitrary"), FFN_VMEM_MIB),
        name="ffn",
    )(*args)


def _inproj_kernel(x_ref, mod_ref, g_ref, w_ref, o_ref, h_ref):
    @pl.when(pl.program_id(1) == 0)
    def _():
        _ada_norm_rows(x_ref, h_ref, g_ref[...], mod_ref[3:4, :], mod_ref[4:5, :])

    o_ref[...] = _dot(h_ref[...], w_ref[...])


def _inproj(x, modl, g, w_in, cond_of):
    n, d = x.shape
    nc = w_in.shape[1]
    tm = cond_of.tile(1024)
    tn = _tile(nc, 1024)
    return pl.pallas_call(
        _inproj_kernel,
        grid=(n // tm, nc // tn),
        in_specs=[
            pl.BlockSpec((tm, d), lambda i, j: (i, 0)),
            pl.BlockSpec((None, N_MOD, d), lambda i, j: (cond_of(i, tm), 0, 0)),
            pl.BlockSpec((1, d), lambda i, j: (0, 0)),
            pl.BlockSpec((d, tn), lambda i, j: (0, j)),
        ],
        out_specs=pl.BlockSpec((tm, tn), lambda i, j: (i, j)),
        out_shape=jax.ShapeDtypeStruct((n, nc), F32),
        scratch_shapes=[pltpu.VMEM((tm, d), BF16)],
        compiler_params=_params(("arbitrary", "arbitrary"), 48),
        name="in_proj",
    )(x, modl, g, w_in)


def _mix_a_kernel(u_ref, v_ref, g_ref, ws_ref, bias_ref, o_ref, vv_ref, *, nchunk):
    v = jax.nn.gelu(v_ref[...])
    vv = v * lax.rsqrt(jnp.mean(v * v, axis=-1, keepdims=True) + EPS) * g_ref[...]
    vv_ref[...] = vv.astype(BF16)
    for n in range(nchunk):
        rows = slice(n * CHUNK, (n + 1) * CHUNK)
        for h in range(A_HEADS):
            cols = slice(h * CHUNK, (h + 1) * CHUNK)
            sp = _dot(ws_ref[h], vv_ref[rows, cols]) + bias_ref[:, cols]
            o_ref[rows, cols] = (jax.nn.gelu(u_ref[rows, cols]) * sp).astype(BF16)


def _mix_a(proj, a_norm_g, a_ws, bias):
    n = proj.shape[0]
    t = 512
    return pl.pallas_call(
        functools.partial(_mix_a_kernel, nchunk=t // CHUNK),
        grid=(n // t,),
        in_specs=[
            pl.BlockSpec((t, GROUP_W), lambda i: (i, 0)),
            pl.BlockSpec((t, GROUP_W), lambda i: (i, 1)),
            pl.BlockSpec((1, GROUP_W), lambda i: (0, 0)),
            pl.BlockSpec((A_HEADS, CHUNK, CHUNK), lambda i: (0, 0, 0)),
            pl.BlockSpec((CHUNK, GROUP_W), lambda i: (0, 0)),
        ],
        out_specs=pl.BlockSpec((t, GROUP_W), lambda i: (i, 0)),
        out_shape=jax.ShapeDtypeStruct((n, GROUP_W), BF16),
        scratch_shapes=[pltpu.VMEM((t, GROUP_W), BF16)],
        compiler_params=_params(("arbitrary",), 32),
        name="mix_a",
    )(proj, proj, a_norm_g, a_ws, bias)


def _mix_b_kernel(ap_ref, ac_ref, an_ref, gp_ref, gc_ref, gn_ref, w_ref, cb_ref, lg_ref, lb_ref,
                  pw_ref, o_ref, pad_ref, rot_ref, conv_ref, *, t, n_prompt, seq_p, seq_s):
    row0 = pl.program_id(0) * t
    in_prompt = row0 < n_prompt
    off = jnp.where(in_prompt, row0, row0 - n_prompt)
    seq = jnp.where(in_prompt, seq_p, seq_s)
    has_prev = lax.rem(off, seq) != 0
    has_next = lax.rem(off + t, seq) != 0

    def glu(a_ref, g_ref):
        return a_ref[...] * jax.nn.sigmoid(g_ref[...])

    pad_ref[0:CONV_HALO, :] = jnp.where(has_prev, glu(ap_ref, gp_ref), 0.0)
    pad_ref[CONV_HALO:CONV_HALO + t, :] = glu(ac_ref, gc_ref)
    pad_ref[CONV_HALO + t:2 * CONV_HALO + t, :] = jnp.where(has_next, glu(an_ref, gn_ref), 0.0)

    span = rot_ref.shape[1]
    for p in range(1, SUBLANES):
        rot_ref[p - 1, :, :] = pad_ref[p:p + span, :]

    base = CONV_HALO - CONV_W // 2
    for r in range(t // CONV_SUB):
        acc = jnp.broadcast_to(cb_ref[...], (CONV_SUB, GROUP_W))
        for j in range(CONV_W):
            p = (base + j) % SUBLANES
            start = r * CONV_SUB + base + j - p
            tap = pad_ref[start:start + CONV_SUB, :] if p == 0 else rot_ref[p - 1, start:start + CONV_SUB, :]
            acc = acc + tap * w_ref[j:j + 1, :]
        conv_ref[r * CONV_SUB:(r + 1) * CONV_SUB, :] = acc

    c = conv_ref[...]
    mu = jnp.mean(c, axis=-1, keepdims=True)
    dlt = c - mu
    y = dlt * lax.rsqrt(jnp.mean(dlt * dlt, axis=-1, keepdims=True) + EPS) * lg_ref[...] + lb_ref[...]
    o_ref[...] = _dot(jax.nn.silu(y).astype(BF16), pw_ref[...]).astype(BF16)


def _mix_b(proj, conv_w, conv_b, ln_g, ln_b, pw, n_prompt, seq_p, seq_s):
    n = proj.shape[0]
    t = _tile(math.gcd(seq_p, seq_s), 256)
    hb = t // CONV_HALO
    nhb = n // CONV_HALO
    cur = lambda c: pl.BlockSpec((t, GROUP_W), lambda i: (i, c))
    prev = lambda c: pl.BlockSpec((CONV_HALO, GROUP_W), lambda i: (jnp.maximum(i * hb - 1, 0), c))
    nxt = lambda c: pl.BlockSpec((CONV_HALO, GROUP_W), lambda i: (jnp.minimum((i + 1) * hb, nhb - 1), c))
    vec = pl.BlockSpec((1, GROUP_W), lambda i: (0, 0))
    return pl.pallas_call(
        functools.partial(_mix_b_kernel, t=t, n_prompt=n_prompt, seq_p=seq_p, seq_s=seq_s),
        grid=(n // t,),
        in_specs=[prev(2), cur(2), nxt(2), prev(3), cur(3), nxt(3),
                  pl.BlockSpec((CONV_W, GROUP_W), lambda i: (0, 0)), vec, vec, vec,
                  pl.BlockSpec((GROUP_W, GROUP_W), lambda i: (0, 0))],
        out_specs=pl.BlockSpec((t, GROUP_W), lambda i: (i, 0)),
        out_shape=jax.ShapeDtypeStruct((n, GROUP_W), BF16),
        scratch_shapes=[pltpu.VMEM((t + 2 * CONV_HALO, GROUP_W), F32),
                        pltpu.VMEM((SUBLANES - 1, t + 2 * CONV_HALO - SUBLANES, GROUP_W), F32),
                        pltpu.VMEM((t, GROUP_W), F32)],
        compiler_params=_params(("arbitrary",), 32),
        name="mix_b",
    )(proj, proj, proj, proj, proj, proj, conv_w, conv_b, ln_g, ln_b, pw)


def _group_mean_sq(x, gm_ref):
    s = x * x
    s_hi = s.astype(BF16)
    s_lo = (s - s_hi.astype(F32)).astype(BF16)
    return _dot(s_hi, gm_ref[...]) + _dot(s_lo, gm_ref[...])


def _swap16(x):
    n = x.shape[-1]
    lane = lax.broadcasted_iota(jnp.int32, x.shape, x.ndim - 1)
    first = lax.rem(lane, 32) < 16
    return jnp.where(first, pltpu.roll(x, n - 16, x.ndim - 1), pltpu.roll(x, 16, x.ndim - 1))


def _prep_prompt_kernel(q_ref, k_ref, v_ref, gq_ref, gk_ref, gm_ref, qo_ref, ko_ref, vo_ref, nk_ref, nv_ref):
    q = q_ref[...]
    k = k_ref[...]
    v = v_ref[...]
    qn = q * lax.rsqrt(_group_mean_sq(q, gm_ref) + EPS) * gq_ref[...]
    kn = k * lax.rsqrt(_group_mean_sq(k, gm_ref) + EPS) * gk_ref[...]
    nk_ref[...] = kn
    nv_ref[...] = v
    qo_ref[...] = (qn * Q_SCALE).astype(BF16)
    ko_ref[...] = kn.astype(BF16)
    vo_ref[...] = v.astype(BF16)


def _prep_latent_kernel(q_ref, k_ref, v_ref, gq_ref, gk_ref, gm_ref, cos_ref, sin_ref, qo_ref, ko_ref, vo_ref):
    q = q_ref[...]
    k = k_ref[...]
    qn = q * lax.rsqrt(_group_mean_sq(q, gm_ref) + EPS) * gq_ref[...]
    kn = k * lax.rsqrt(_group_mean_sq(k, gm_ref) + EPS) * gk_ref[...]
    cos = cos_ref[...]
    sin = sin_ref[...]
    qr = qn * cos + _swap16(qn) * sin
    kr = kn * cos + _swap16(kn) * sin
    qo_ref[...] = (qr * Q_SCALE).astype(BF16)
    ko_ref[...] = kr.astype(BF16)
    vo_ref[...] = v_ref[...].astype(BF16)


def _prep(proj, gq, gk, gm, row0, nrows, rope):
    t = _tile(nrows, 512)
    rb = row0 // t
    col = lambda c: pl.BlockSpec((t, GROUP_W), lambda i: (rb + i, c))
    vec = pl.BlockSpec((1, GROUP_W), lambda i: (0, 0))
    out = pl.BlockSpec((t, GROUP_W), lambda i: (i, 0))
    in_specs = [col(4), col(5), col(6), vec, vec, pl.BlockSpec((GROUP_W, GROUP_W), lambda i: (0, 0))]
    args = [proj, proj, proj, gq, gk, gm]
    shp = lambda dt: jax.ShapeDtypeStruct((nrows, GROUP_W), dt)
    if rope is None:
        body, out_specs, out_shape = _prep_prompt_kernel, [out] * 5, [shp(BF16)] * 3 + [shp(F32)] * 2
    else:
        cos, sin = rope
        npos = cos.shape[0] // t
        tab = pl.BlockSpec((t, GROUP_W), lambda i: (lax.rem(i, npos), 0))
        in_specs += [tab, tab]
        args += [cos, sin]
        body, out_specs, out_shape = _prep_latent_kernel, [out] * 3, [shp(BF16)] * 3
    return pl.pallas_call(
        body, grid=(nrows // t,), in_specs=in_specs, out_specs=out_specs, out_shape=out_shape,
        compiler_params=_params(("arbitrary",), 32),
        name="qkv_prep_prompt" if rope is None else "qkv_prep_latent",
    )(*args)


def _attn_kernel(*refs, lam_init):
    lam_ref, q_ref, k_ref, v_ref, sg_ref = refs[:5]
    o_ref = refs[-1]
    lp = lam_ref[...]
    lam = (jnp.exp(jnp.sum(lp[0:1, :] * lp[1:2, :], axis=-1, keepdims=True))
           - jnp.exp(jnp.sum(lp[2:3, :] * lp[3:4, :], axis=-1, keepdims=True)) + lam_init)
    tq = q_ref.shape[0]
    lane = lax.broadcasted_iota(jnp.int32, (tq, C_DV), 1)
    first = lane < C_DK
    dims = (((1,), (1,)), ((), ()))

    def exp_sum(s):
        e = jnp.exp2(s - jnp.max(s, axis=-1, keepdims=True))
        return e, jnp.sum(e, axis=-1, keepdims=True)

    for h in range(C_HEADS):
        cols = slice(h * C_DV, (h + 1) * C_DV)
        qh = q_ref[:, cols]
        kh = k_ref[:, cols]
        zero = jnp.zeros_like(qh)
        e1, l1 = exp_sum(lax.dot_general(jnp.where(first, qh, zero), kh, dims, preferred_element_type=F32))
        e2, l2 = exp_sum(lax.dot_general(jnp.where(first, zero, qh), kh, dims, preferred_element_type=F32))
        r = lam * l1 * (1.0 / l2)
        a = (e1 - r * e2).astype(BF16)
        o = _dot(a, v_ref[:, cols]) * (1.0 / l1)
        o = o * lax.rsqrt(jnp.mean(o * o, axis=-1, keepdims=True) + EPS) * sg_ref[...]
        o_ref[:, cols] = (o * (1.0 - lam_init)).astype(BF16)


def _attention(c_lambda, q, k, v, subln_g, lam_init, n_total, row0, prev_out):
    nb, lk, _ = k.shape
    lq = q.shape[0] // nb
    tq = _tile(lq, 256)
    nq = lq // tq
    rb = row0 // tq
    in_specs = [
        pl.BlockSpec((4, C_DK), lambda b, t: (0, 0)),
        pl.BlockSpec((tq, GROUP_W), lambda b, t: (b * nq + t, 0)),
        pl.BlockSpec((None, lk, GROUP_W), lambda b, t: (b, 0, 0)),
        pl.BlockSpec((None, lk, GROUP_W), lambda b, t: (b, 0, 0)),
        pl.BlockSpec((1, C_DV), lambda b, t: (0, 0)),
    ]
    args = [c_lambda, q, k, v, subln_g]
    aliases = {}
    if prev_out is not None:
        in_specs.append(pl.BlockSpec(memory_space=pl.ANY))
        args.append(prev_out)
        aliases = {5: 0}
    return pl.pallas_call(
        functools.partial(_attn_kernel, lam_init=lam_init),
        grid=(nb, nq),
        in_specs=in_specs,
        out_specs=pl.BlockSpec((tq, GROUP_W), lambda b, t: (rb + b * nq + t, 0)),
        out_shape=jax.ShapeDtypeStruct((n_total, GROUP_W), BF16),
        input_output_aliases=aliases,
        compiler_params=_params(("arbitrary", "arbitrary"), 56),
        name="diff_attn",
    )(*args)


def _dft_kernel(*refs, scale):
    dx_ref, wc_ref, c_ref, s_ref, dl_ref = refs[:5]
    o_ref, pq_ref = refs[-2], refs[-1]

    @pl.when(pl.program_id(1) == 0)
    def _():
        pq_ref[...] = _dot(dx_ref[...].astype(BF16), wc_ref[...]).astype(BF16)

    y = _dot(c_ref[...], pq_ref[:, 0:GROUP_W]) - _dot(s_ref[...], pq_ref[:, GROUP_W:2 * GROUP_W])
    o_ref[...] = _dot((y * scale).astype(BF16), dl_ref[...]).astype(BF16)


def _fourier(proj, wc, cl, sl, d_lin, nb, seq, n_total, row0, prev_out):
    tr = _tile(seq, 512)
    nr = seq // tr
    sb = row0 // seq
    rb = row0 // tr
    in_specs = [
        pl.BlockSpec((seq, GROUP_W), lambda b, r: (sb + b, N_GROUPS - 1)),
        pl.BlockSpec((GROUP_W, 2 * GROUP_W), lambda b, r: (0, 0)),
        pl.BlockSpec((tr, seq), lambda b, r: (r, 0)),
        pl.BlockSpec((tr, seq), lambda b, r: (r, 0)),
        pl.BlockSpec((GROUP_W, GROUP_W), lambda b, r: (0, 0)),
    ]
    args = [proj, wc, cl, sl, d_lin]
    aliases = {}
    if prev_out is not None:
        in_specs.append(pl.BlockSpec(memory_space=pl.ANY))
        args.append(prev_out)
        aliases = {5: 0}
    return pl.pallas_call(
        functools.partial(_dft_kernel, scale=1.0 / math.sqrt(seq * D_GW)),
        grid=(nb, nr),
        in_specs=in_specs,
        out_specs=pl.BlockSpec((tr, GROUP_W), lambda b, r: (rb + b * nr + r, 0)),
        out_shape=jax.ShapeDtypeStruct((n_total, GROUP_W), BF16),
        scratch_shapes=[pltpu.VMEM((seq, 2 * GROUP_W), BF16)],
        input_output_aliases=aliases,
        compiler_params=_params(("arbitrary", "arbitrary"), 56),
        name="fourier_mix",
    )(*args)


def _outproj_kernel(x_ref, mod_ref, a_ref, b_ref, c_ref, d_ref, w_ref, o_ref):
    y = _dot(a_ref[...], w_ref[0:GROUP_W, :])
    y += _dot(b_ref[...], w_ref[GROUP_W:2 * GROUP_W, :])
    y += _dot(c_ref[...], w_ref[2 * GROUP_W:3 * GROUP_W, :])
    y += _dot(d_ref[...], w_ref[3 * GROUP_W:4 * GROUP_W, :])
    o_ref[...] = x_ref[...] + mod_ref[5:6, :] * y


def _outproj(x, modl, oa, ob, oc, od, w_out, cond_of):
    n, d = x.shape
    tm = cond_of.tile(512)
    mix = pl.BlockSpec((tm, GROUP_W), lambda i: (i, 0))
    return pl.pallas_call(
        _outproj_kernel,
        grid=(n // tm,),
        in_specs=[
            pl.BlockSpec((tm, d), lambda i: (i, 0)),
            pl.BlockSpec((None, N_MOD, d), lambda i: (cond_of(i, tm), 0, 0)),
            mix, mix, mix, mix,
            pl.BlockSpec((4 * GROUP_W, d), lambda i: (0, 0)),
        ],
        out_specs=pl.BlockSpec((tm, d), lambda i: (i, 0)),
        out_shape=jax.ShapeDtypeStruct((n, d), F32),
        compiler_params=_params(("arbitrary",), 48),
        name="out_proj",
    )(x, modl, oa, ob, oc, od, w_out)


def _dft_tables(n):
    j = lax.broadcasted_iota(jnp.int32, (n, n), 0)
    k = lax.broadcasted_iota(jnp.int32, (n, n), 1)
    ang = (2.0 * math.pi / n) * lax.rem(j * k, n).astype(F32)
    return jnp.cos(ang), jnp.sin(ang)


def _channel_dft_table():
    c, s = _dft_tables(D_GW)
    eye = jnp.eye(GROUP_W // D_GW, dtype=F32)
    return jnp.concatenate([jnp.kron(eye, c), jnp.kron(eye, s)], axis=1).astype(BF16)


def _rope_tables(n_tokens):
    rows = n_tokens // GRID_W
    row = jnp.repeat(jnp.arange(rows, dtype=F32), GRID_W)
    col = jnp.tile(jnp.arange(GRID_W, dtype=F32), rows)
    freqs = ROPE_THETA ** (-jnp.arange(ROPE_NF, dtype=F32) / ROPE_NF)
    ang_r = row[:, None] * freqs
    ang_c = col[:, None] * freqs

    def lanes(fr, fc, sign):
        grp = jnp.concatenate([sign * fr, fr, sign * fc, fc], axis=-1)
        return jnp.tile(grp, (1, GROUP_W // C_DK))

    cos = lanes(jnp.cos(ang_r), jnp.cos(ang_c), 1.0)
    sin = lanes(jnp.sin(ang_r), jnp.sin(ang_c), -1.0)
    return cos, sin


class _CondOf:
    def __init__(self, n_prompt, seq_s):
        self.n_prompt = n_prompt
        self.seq_s = seq_s

    def tile(self, target):
        return _tile(math.gcd(self.n_prompt, self.seq_s), target)

    def __call__(self, i, tm):
        r = i * tm
        return jnp.where(r < self.n_prompt, 0, 1 + (r - self.n_prompt) // self.seq_s)


def kernel(x_prompt, x_sample, c, cache_k, cache_v, c_ctx, w_mod, b_mod, norm_g, w_ff1_in, w_ff1_down, w_ff2_in, w_ff2_down, w_in, w_out, a_norm_g, a_ws, a_bs, b_conv_w, b_conv_b, b_ln_g, b_ln_b, b_pw, c_qnorm_g, c_knorm_g, c_lambda, c_subln_g, d_lin):
    nbp, seq_p, d = x_prompt.shape
    nbs, seq_s, _ = x_sample.shape
    depth = w_mod.shape[0]
    past = cache_k.shape[2]
    n_p, n_s = nbp * seq_p, nbs * seq_s
    n = n_p + n_s
    assert n_p % seq_s == 0 and seq_p % CHUNK == 0 and seq_s % CHUNK == 0
    assert 1 + nbs <= COND_PAD
    cond_of = _CondOf(n_p, seq_s)

    cond = jnp.zeros((COND_PAD, d), F32).at[0].set(c_ctx).at[1:1 + nbs].set(c)
    mods = _modulation(cond, w_mod, b_mod).reshape(depth, COND_PAD, N_MOD, d)

    wc = _channel_dft_table()
    tabs_p = [t.astype(BF16) for t in _dft_tables(seq_p)]
    tabs_s = [t.astype(BF16) for t in _dft_tables(seq_s)]
    rope = _rope_tables(seq_s)
    gidx = jnp.arange(GROUP_W) // C_DK
    gm = jnp.where(gidx[:, None] == gidx[None, :], 1.0 / C_DK, 0.0).astype(BF16)

    ks_out, vs_out = [], []
    for l in range(depth):
        lam_init = 0.8 - 0.6 * math.exp(-0.3 * l)
        modl = mods[l]
        ng = norm_g[l]
        ff1 = (modl, ng[0:1], w_ff1_in[l].astype(BF16), w_ff1_down[l].astype(BF16), 0, cond_of)
        if l == 0:
            x = _ffn(x_prompt.reshape(n_p, d), *ff1, out_total=n)
            x = _ffn(x_sample.reshape(n_s, d), *ff1, tok_row0=n_p, out_total=n, out_row0=n_p, prev_out=x)
        else:
            x = _ffn(x, *ff1)

        proj = _inproj(x, modl, ng[1:2], w_in[l].astype(BF16), cond_of)

        bias_a = jnp.repeat(a_bs[l].T, CHUNK, axis=1)
        out_a = _mix_a(proj, a_norm_g[l][None], a_ws[l].astype(BF16), bias_a)

        out_b = _mix_b(proj, b_conv_w[l], b_conv_b[l][None], b_ln_g[l][None], b_ln_b[l][None],
                       b_pw[l].astype(BF16), n_p, seq_p, seq_s)

        gq = jnp.tile(c_qnorm_g[l], GROUP_W // C_DK)[None]
        gk = jnp.tile(c_knorm_g[l], GROUP_W // C_DK)[None]
        sg = c_subln_g[l][None]
        qp, kp, vp, nk, nv = _prep(proj, gq, gk, gm, 0, n_p, None)
        ks_out.append(nk.reshape(nbp, seq_p, C_HEADS, 2 * C_DK))
        vs_out.append(nv.reshape(nbp, seq_p, C_HEADS, C_DV))
        qs, ksl, vsl = _prep(proj, gq, gk, gm, n_p, n_s, rope)
        k_all = jnp.concatenate([cache_k[:, l].reshape(nbs, past, GROUP_W).astype(BF16),
                                 ksl.reshape(nbs, seq_s, GROUP_W)], axis=1)
        v_all = jnp.concatenate([cache_v[:, l].reshape(nbs, past, GROUP_W).astype(BF16),
                                 vsl.reshape(nbs, seq_s, GROUP_W)], axis=1)
        out_c = _attention(c_lambda[l], qp, kp.reshape(nbp, seq_p, GROUP_W), vp.reshape(nbp, seq_p, GROUP_W),
                           sg, lam_init, n, 0, None)
        out_c = _attention(c_lambda[l], qs, k_all, v_all, sg, lam_init, n, n_p, out_c)

        dl = d_lin[l].astype(BF16)
        out_d = _fourier(proj, wc, tabs_p[0], tabs_p[1], dl, nbp, seq_p, n, 0, None)
        out_d = _fourier(proj, wc, tabs_s[0], tabs_s[1], dl, nbs, seq_s, n, n_p, out_d)

        x = _outproj(x, modl, out_a, out_b, out_c, out_d, w_out[l].astype(BF16), cond_of)
        ff2 = (modl, ng[2:3], w_ff2_in[l].astype(BF16), w_ff2_down[l].astype(BF16), 6, cond_of)
        if l == depth - 1:
            y_p = _ffn(x, *ff2, nrows=n_p)
            y_s = _ffn(x, *ff2, in_row0=n_p, nrows=n_s, tok_row0=n_p)
        else:
            x = _ffn(x, *ff2)

    new_k = jnp.stack(ks_out, axis=1)
    new_v = jnp.stack(vs_out, axis=1)
    return (y_p.reshape(nbp, seq_p, d), y_s.reshape(nbs, seq_s, d), new_k, new_v)
```

```python
import functools
import math

import jax
import jax.numpy as jnp
from jax import lax
from jax.experimental import pallas as pl
from jax.experimental.pallas import tpu as pltpu

F32 = jnp.float32
BF16 = jnp.bfloat16

EPS = 1e-6
GROUP_W = 512
N_GROUPS = 8
CHUNK = 128
A_HEADS = 4
CONV_W = 31
CONV_HALO = 16
CONV_SUB = 32
SUBLANES = 8
C_HEADS = 4
C_DK = 64
C_DV = 128
Q_SCALE = C_DK ** -0.5 * math.log2(math.e)
ATTN_KEY_CHUNK = 512
ROPE_NF = 16
ROPE_THETA = 10000.0
GRID_W = 64
D_GW = 128
N_MOD = 9
COND_PAD = 8
MIB = 1024 * 1024
FFN_TM = 1024
FFN_NSPLIT = 4
FFN_VMEM_MIB = 60
NORM_ROWS = 16
NORM_UNROLL = 8


def _tile(n, target):
    t = target
    while n % t:
        t //= 2
    return t


def _params(sem, vmem_mib):
    return pltpu.CompilerParams(dimension_semantics=sem, vmem_limit_bytes=vmem_mib * MIB)


def _dot(a, b):
    return jnp.dot(a, b, preferred_element_type=F32)


def _mod_kernel(c_ref, w_ref, b_ref, o_ref):
    s = jax.nn.silu(c_ref[...]).astype(BF16)
    o_ref[...] = _dot(s, w_ref[...].astype(BF16)) + b_ref[...]


def _modulation(cond, w_mod, b_mod):
    depth, d, nd = w_mod.shape
    tn = _tile(nd, 1024)
    return pl.pallas_call(
        _mod_kernel,
        grid=(depth, nd // tn),
        in_specs=[
            pl.BlockSpec((COND_PAD, d), lambda l, j: (0, 0)),
            pl.BlockSpec((None, d, tn), lambda l, j: (l, 0, j)),
            pl.BlockSpec((None, 1, tn), lambda l, j: (l, 0, j)),
        ],
        out_specs=pl.BlockSpec((None, COND_PAD, tn), lambda l, j: (l, 0, j)),
        out_shape=jax.ShapeDtypeStruct((depth, COND_PAD, nd), F32),
        compiler_params=_params(("arbitrary", "arbitrary"), 40),
        name="modulation",
    )(cond, w_mod, b_mod.reshape(depth, 1, nd))


def _ada_norm_rows(x_ref, h_ref, g, shift, scale, copy_ref=None):
    gain = g * (1.0 + scale)

    def body(c, carry):
        rows = pl.ds(pl.multiple_of(c * NORM_ROWS, NORM_ROWS), NORM_ROWS)
        x = x_ref[rows, :]
        y = x * lax.rsqrt(jnp.mean(x * x, axis=-1, keepdims=True) + EPS)
        h_ref[rows, :] = (y * gain + shift).astype(BF16)
        if copy_ref is not None:
            copy_ref[rows, :] = x
        return carry

    lax.fori_loop(0, x_ref.shape[0] // NORM_ROWS, body, 0, unroll=NORM_UNROLL)


def _ffn_kernel(*refs, s, n_split):
    x_ref, mod_ref, g_ref, wa_ref, wg_ref, wd_ref = refs[:6]
    o_ref, h_ref = refs[-2:]

    @pl.when(pl.program_id(1) == 0)
    def _():
        _ada_norm_rows(x_ref, h_ref, g_ref[...], mod_ref[s:s + 1, :], mod_ref[s + 1:s + 2, :], copy_ref=o_ref)

    h = h_ref[...]
    a = _dot(h, wa_ref[...])
    g = _dot(h, wg_ref[...])
    act = (jax.nn.silu(g) * a).astype(BF16)
    gate = 0.5 * mod_ref[s + 2:s + 3, :]
    cw = o_ref.shape[1] // n_split
    for c in range(n_split):
        cols = slice(c * cw, (c + 1) * cw)
        o_ref[:, cols] += gate[:, cols] * _dot(act, wd_ref[:, cols])


def _ffn(x, modl, g, w_up, w_down, s, cond_of, *, in_row0=0, nrows=None, tok_row0=0,
         out_total=None, out_row0=0, prev_out=None):
    d = x.shape[1]
    nrows = x.shape[0] if nrows is None else nrows
    out_total = nrows if out_total is None else out_total
    ff = w_down.shape[0]
    tm = cond_of.tile(FFN_TM)
    tf = _tile(ff, 512)
    nf = ff // tf
    ib, tb, ob = in_row0 // tm, tok_row0 // tm, out_row0 // tm
    in_specs = [
        pl.BlockSpec((tm, d), lambda i, f: (ib + i, 0)),
        pl.BlockSpec((None, N_MOD, d), lambda i, f: (cond_of(tb + i, tm), 0, 0)),
        pl.BlockSpec((1, d), lambda i, f: (0, 0)),
        pl.BlockSpec((d, tf), lambda i, f: (0, f)),
        pl.BlockSpec((d, tf), lambda i, f: (0, f + nf)),
        pl.BlockSpec((tf, d), lambda i, f: (f, 0)),
    ]
    args = [x, modl, g, w_up, w_up, w_down]
    aliases = {}
    if prev_out is not None:
        in_specs.append(pl.BlockSpec(memory_space=pl.ANY))
        args.append(prev_out)
        aliases = {6: 0}
    return pl.pallas_call(
        functools.partial(_ffn_kernel, s=s, n_split=FFN_NSPLIT),
        grid=(nrows // tm, nf),
        in_specs=in_specs,
        out_specs=pl.BlockSpec((tm, d), lambda i, f: (ob + i, 0)),
        out_shape=jax.ShapeDtypeStruct((out_total, d), F32),
        scratch_shapes=[pltpu.VMEM((tm, d), BF16)],
        input_output_aliases=aliases,
        compiler_params=_params(("arbitrary", "arbitrary"), FFN_VMEM_MIB),
        name="ffn",
    )(*args)


def _inproj_kernel(x_ref, mod_ref, g_ref, w_ref, o_ref, h_ref):
    @pl.when(pl.program_id(1) == 0)
    def _():
        _ada_norm_rows(x_ref, h_ref, g_ref[...], mod_ref[3:4, :], mod_ref[4:5, :])

    o_ref[...] = _dot(h_ref[...], w_ref[...])


def _inproj(x, modl, g, w_in, cond_of):
    n, d = x.shape
    nc = w_in.shape[1]
    tm = cond_of.tile(1024)
    tn = _tile(nc, 1024)
    return pl.pallas_call(
        _inproj_kernel,
        grid=(n // tm, nc // tn),
        in_specs=[
            pl.BlockSpec((tm, d), lambda i, j: (i, 0)),
            pl.BlockSpec((None, N_MOD, d), lambda i, j: (cond_of(i, tm), 0, 0)),
            pl.BlockSpec((1, d), lambda i, j: (0, 0)),
            pl.BlockSpec((d, tn), lambda i, j: (0, j)),
        ],
        out_specs=pl.BlockSpec((tm, tn), lambda i, j: (i, j)),
        out_shape=jax.ShapeDtypeStruct((n, nc), F32),
        scratch_shapes=[pltpu.VMEM((tm, d), BF16)],
        compiler_params=_params(("arbitrary", "arbitrary"), 48),
        name="in_proj",
    )(x, modl, g, w_in)


def _mix_a_kernel(u_ref, v_ref, g_ref, ws_ref, bias_ref, o_ref, vv_ref, *, nchunk):
    v = jax.nn.gelu(v_ref[...])
    vv = v * lax.rsqrt(jnp.mean(v * v, axis=-1, keepdims=True) + EPS) * g_ref[...]
    vv_ref[...] = vv.astype(BF16)
    for n in range(nchunk):
        rows = slice(n * CHUNK, (n + 1) * CHUNK)
        for h in range(A_HEADS):
            cols = slice(h * CHUNK, (h + 1) * CHUNK)
            sp = _dot(ws_ref[h], vv_ref[rows, cols]) + bias_ref[:, cols]
            o_ref[rows, cols] = (jax.nn.gelu(u_ref[rows, cols]) * sp).astype(BF16)


def _mix_a(proj, a_norm_g, a_ws, bias):
    n = proj.shape[0]
    t = 512
    return pl.pallas_call(
        functools.partial(_mix_a_kernel, nchunk=t // CHUNK),
        grid=(n // t,),
        in_specs=[
            pl.BlockSpec((t, GROUP_W), lambda i: (i, 0)),
            pl.BlockSpec((t, GROUP_W), lambda i: (i, 1)),
            pl.BlockSpec((1, GROUP_W), lambda i: (0, 0)),
            pl.BlockSpec((A_HEADS, CHUNK, CHUNK), lambda i: (0, 0, 0)),
            pl.BlockSpec((CHUNK, GROUP_W), lambda i: (0, 0)),
        ],
        out_specs=pl.BlockSpec((t, GROUP_W), lambda i: (i, 0)),
        out_shape=jax.ShapeDtypeStruct((n, GROUP_W), BF16),
        scratch_shapes=[pltpu.VMEM((t, GROUP_W), BF16)],
        compiler_params=_params(("arbitrary",), 32),
        name="mix_a",
    )(proj, proj, a_norm_g, a_ws, bias)


def _mix_b_kernel(ap_ref, ac_ref, an_ref, gp_ref, gc_ref, gn_ref, w_ref, cb_ref, lg_ref, lb_ref,
                  pw_ref, o_ref, pad_ref, rot_ref, conv_ref, *, t, n_prompt, seq_p, seq_s):
    row0 = pl.program_id(0) * t
    in_prompt = row0 < n_prompt
    off = jnp.where(in_prompt, row0, row0 - n_prompt)
    seq = jnp.where(in_prompt, seq_p, seq_s)
    has_prev = lax.rem(off, seq) != 0
    has_next = lax.rem(off + t, seq) != 0

    def glu(a_ref, g_ref):
        return a_ref[...] * jax.nn.sigmoid(g_ref[...])

    pad_ref[0:CONV_HALO, :] = jnp.where(has_prev, glu(ap_ref, gp_ref), 0.0)
    pad_ref[CONV_HALO:CONV_HALO + t, :] = glu(ac_ref, gc_ref)
    pad_ref[CONV_HALO + t:2 * CONV_HALO + t, :] = jnp.where(has_next, glu(an_ref, gn_ref), 0.0)

    span = rot_ref.shape[1]
    for p in range(1, SUBLANES):
        rot_ref[p - 1, :, :] = pad_ref[p:p + span, :]

    base = CONV_HALO - CONV_W // 2
    for r in range(t // CONV_SUB):
        acc = jnp.broadcast_to(cb_ref[...], (CONV_SUB, GROUP_W))
        for j in range(CONV_W):
            p = (base + j) % SUBLANES
            start = r * CONV_SUB + base + j - p
            tap = pad_ref[start:start + CONV_SUB, :] if p == 0 else rot_ref[p - 1, start:start + CONV_SUB, :]
            acc = acc + tap * w_ref[j:j + 1, :]
        conv_ref[r * CONV_SUB:(r + 1) * CONV_SUB, :] = acc

    c = conv_ref[...]
    mu = jnp.mean(c, axis=-1, keepdims=True)
    dlt = c - mu
    y = dlt * lax.rsqrt(jnp.mean(dlt * dlt, axis=-1, keepdims=True) + EPS) * lg_ref[...] + lb_ref[...]
    o_ref[...] = _dot(jax.nn.silu(y).astype(BF16), pw_ref[...]).astype(BF16)


def _mix_b(proj, conv_w, conv_b, ln_g, ln_b, pw, n_prompt, seq_p, seq_s):
    n = proj.shape[0]
    t = _tile(math.gcd(seq_p, seq_s), 256)
    hb = t // CONV_HALO
    nhb = n // CONV_HALO
    cur = lambda c: pl.BlockSpec((t, GROUP_W), lambda i: (i, c))
    prev = lambda c: pl.BlockSpec((CONV_HALO, GROUP_W), lambda i: (jnp.maximum(i * hb - 1, 0), c))
    nxt = lambda c: pl.BlockSpec((CONV_HALO, GROUP_W), lambda i: (jnp.minimum((i + 1) * hb, nhb - 1), c))
    vec = pl.BlockSpec((1, GROUP_W), lambda i: (0, 0))
    return pl.pallas_call(
        functools.partial(_mix_b_kernel, t=t, n_prompt=n_prompt, seq_p=seq_p, seq_s=seq_s),
        grid=(n // t,),
        in_specs=[prev(2), cur(2), nxt(2), prev(3), cur(3), nxt(3),
                  pl.BlockSpec((CONV_W, GROUP_W), lambda i: (0, 0)), vec, vec, vec,
                  pl.BlockSpec((GROUP_W, GROUP_W), lambda i: (0, 0))],
        out_specs=pl.BlockSpec((t, GROUP_W), lambda i: (i, 0)),
        out_shape=jax.ShapeDtypeStruct((n, GROUP_W), BF16),
        scratch_shapes=[pltpu.VMEM((t + 2 * CONV_HALO, GROUP_W), F32),
                        pltpu.VMEM((SUBLANES - 1, t + 2 * CONV_HALO - SUBLANES, GROUP_W), F32),
                        pltpu.VMEM((t, GROUP_W), F32)],
        compiler_params=_params(("arbitrary",), 32),
        name="mix_b",
    )(proj, proj, proj, proj, proj, proj, conv_w, conv_b, ln_g, ln_b, pw)


def _group_mean_sq(x, gm_ref):
    s = x * x
    s_hi = s.astype(BF16)
    s_lo = (s - s_hi.astype(F32)).astype(BF16)
    return _dot(s_hi, gm_ref[...]) + _dot(s_lo, gm_ref[...])


def _swap16(x):
    n = x.shape[-1]
    lane = lax.broadcasted_iota(jnp.int32, x.shape, x.ndim - 1)
    first = lax.rem(lane, 32) < 16
    return jnp.where(first, pltpu.roll(x, n - 16, x.ndim - 1), pltpu.roll(x, 16, x.ndim - 1))


def _prep_prompt_kernel(*refs):
    q_ref, k_ref, v_ref, gq_ref, gk_ref, gm_ref = refs[:6]
    qo_ref, ko_ref, vo_ref, nk_ref, nv_ref = refs[-5:]
    q = q_ref[...]
    k = k_ref[...]
    v = v_ref[...]
    qn = q * lax.rsqrt(_group_mean_sq(q, gm_ref) + EPS) * gq_ref[...]
    kn = k * lax.rsqrt(_group_mean_sq(k, gm_ref) + EPS) * gk_ref[...]
    nseq, seq = nk_ref.shape[0], nk_ref.shape[1]
    for b in range(nseq):
        rows = slice(b * seq, (b + 1) * seq)
        for h in range(C_HEADS):
            cols = slice(h * C_DV, (h + 1) * C_DV)
            nk_ref[b, :, h, :] = kn[rows, cols]
            nv_ref[b, :, h, :] = v[rows, cols]
    qo_ref[...] = (qn * Q_SCALE).astype(BF16)
    ko_ref[...] = kn.astype(BF16)
    vo_ref[...] = v.astype(BF16)


def _prep_latent_kernel(q_ref, k_ref, v_ref, gq_ref, gk_ref, gm_ref, cos_ref, sin_ref, qo_ref, ko_ref, vo_ref):
    q = q_ref[...]
    k = k_ref[...]
    qn = q * lax.rsqrt(_group_mean_sq(q, gm_ref) + EPS) * gq_ref[...]
    kn = k * lax.rsqrt(_group_mean_sq(k, gm_ref) + EPS) * gk_ref[...]
    cos = cos_ref[...]
    sin = sin_ref[...]
    qr = qn * cos + _swap16(qn) * sin
    kr = kn * cos + _swap16(kn) * sin
    qo_ref[...] = (qr * Q_SCALE).astype(BF16)
    ko_ref[...] = kr.astype(BF16)
    vo_ref[...] = v_ref[...].astype(BF16)


def _prep(proj, gq, gk, gm, row0, nrows, rope, past=0, cache_out=None):
    if rope is None:
        t = _tile(nrows, max(512, cache_out[2]))
    else:
        t = _tile(math.gcd(nrows, past), 512)
    rb = row0 // t
    col = lambda c: pl.BlockSpec((t, GROUP_W), lambda i: (rb + i, c))
    vec = pl.BlockSpec((1, GROUP_W), lambda i: (0, 0))
    out = pl.BlockSpec((t, GROUP_W), lambda i: (i, 0))
    in_specs = [col(4), col(5), col(6), vec, vec, pl.BlockSpec((GROUP_W, GROUP_W), lambda i: (0, 0))]
    args = [proj, proj, proj, gq, gk, gm]
    shp = lambda dt: jax.ShapeDtypeStruct((nrows, GROUP_W), dt)
    aliases = {}
    if rope is None:
        layer, depth, seq, prev = cache_out
        nseq = t // seq
        cache = pl.BlockSpec((nseq, None, seq, C_HEADS, C_DV), lambda i: (i, layer, 0, 0, 0))
        cache_shape = jax.ShapeDtypeStruct((nrows // seq, depth, seq, C_HEADS, C_DV), F32)
        body, out_specs, out_shape = _prep_prompt_kernel, [out] * 3 + [cache] * 2, [shp(BF16)] * 3 + [cache_shape] * 2
        if prev is not None:
            in_specs += [pl.BlockSpec(memory_space=pl.ANY)] * 2
            args += list(prev)
            aliases = {6: 3, 7: 4}
    else:
        cos, sin = rope
        seq = cos.shape[0]
        npos = seq // t
        tab = pl.BlockSpec((t, GROUP_W), lambda i: (lax.rem(i, npos), 0))
        in_specs += [tab, tab]
        args += [cos, sin]
        kv = pl.BlockSpec((None, t, GROUP_W), lambda i: (i // npos, past // t + lax.rem(i, npos), 0))
        kv_shape = jax.ShapeDtypeStruct((nrows // seq, past + seq, GROUP_W), BF16)
        body, out_specs, out_shape = _prep_latent_kernel, [out, kv, kv], [shp(BF16), kv_shape, kv_shape]
    return pl.pallas_call(
        body, grid=(nrows // t,), in_specs=in_specs, out_specs=out_specs, out_shape=out_shape,
        input_output_aliases=aliases,
        compiler_params=_params(("arbitrary",), 32),
        name="qkv_prep_prompt" if rope is None else "qkv_prep_latent",
    )(*args)


def _cache_fill_kernel(ck_ref, cv_ref, k_in, v_in, ko_ref, vo_ref):
    for h in range(C_HEADS):
        cols = slice(h * C_DV, (h + 1) * C_DV)
        ko_ref[:, cols] = ck_ref[:, h, :].astype(BF16)
        vo_ref[:, cols] = cv_ref[:, h, :].astype(BF16)


def _cache_fill(cache_k, cache_v, layer, k_all, v_all):
    nbs, _, past, nh, dv = cache_k.shape
    cache = pl.BlockSpec((None, None, past, nh, dv), lambda b: (b, layer, 0, 0, 0))
    out = pl.BlockSpec((None, past, GROUP_W), lambda b: (b, 0, 0))
    hbm = pl.BlockSpec(memory_space=pl.ANY)
    return pl.pallas_call(
        _cache_fill_kernel, grid=(nbs,), in_specs=[cache, cache, hbm, hbm], out_specs=[out, out],
        out_shape=[jax.ShapeDtypeStruct(k_all.shape, BF16)] * 2,
        input_output_aliases={2: 0, 3: 1},
        compiler_params=_params(("arbitrary",), 32),
        name="cache_fill",
    )(cache_k, cache_v, k_all, v_all)


def _attn_kernel(*refs, lam_init):
    lam_ref, q_ref, k_ref, v_ref, sg_ref = refs[:5]
    lp = lam_ref[...]
    lam = (jnp.exp(jnp.sum(lp[0:1, :] * lp[1:2, :], axis=-1, keepdims=True))
           - jnp.exp(jnp.sum(lp[2:3, :] * lp[3:4, :], axis=-1, keepdims=True)) + lam_init)
    s_ref = refs[-1]
    o_ref = refs[-2]
    tq, lk = q_ref.shape[0], k_ref.shape[0]
    kc = _tile(lk, ATTN_KEY_CHUNK)
    lane = lax.broadcasted_iota(jnp.int32, (tq, C_DV), 1)
    first = lane < C_DK
    dims = (((1,), (1,)), ((), ()))
    head_cols = lambda h: slice(h * C_DV, (h + 1) * C_DV)

    def lane_groups(x):
        return [x[:, g * C_DV:(g + 1) * C_DV] for g in range(x.shape[1] // C_DV)]

    row_max, row_sum = {}, {}
    for t in range(C_HEADS + 2):
        ha, hb, hc = t, t - 1, t - 2
        do_a, do_b, do_c = ha < C_HEADS, 0 <= hb < C_HEADS, 0 <= hc
        if do_a:
            qh = q_ref[:, head_cols(ha)]
            zero = jnp.zeros_like(qh)
            q_maps = jnp.concatenate([jnp.where(first, qh, zero), jnp.where(first, zero, qh)], axis=0)
            part_max = [jnp.full((tq, C_DV), -jnp.inf, F32) for _ in range(2)]
        if do_b:
            part_sum = [jnp.zeros((tq, C_DV), F32) for _ in range(2)]
        if do_c:
            l1, l2 = row_sum[hc]
            r = lam * l1 * (1.0 / l2)
            acc = jnp.zeros((tq, C_DV), F32)
        for c in range(lk // kc):
            keys = slice(c * kc, (c + 1) * kc)
            if do_a:
                s_maps = lax.dot_general(q_maps, k_ref[keys, head_cols(ha)], dims, preferred_element_type=F32)
                for m in range(2):
                    s = s_maps[m * tq:(m + 1) * tq, :]
                    s_ref[ha % 3, m, :, keys] = s
                    part_max[m] = jnp.maximum(part_max[m], functools.reduce(jnp.maximum, lane_groups(s)))
            if do_b:
                for m in range(2):
                    e = jnp.exp2(s_ref[hb % 3, m, :, keys] - row_max[hb][m])
                    s_ref[hb % 3, m, :, keys] = e
                    part_sum[m] = part_sum[m] + functools.reduce(jnp.add, lane_groups(e))
            if do_c:
                a = (s_ref[hc % 3, 0, :, keys] - r * s_ref[hc % 3, 1, :, keys]).astype(BF16)
                acc = acc + _dot(a, v_ref[keys, head_cols(hc)])
        if do_a:
            row_max[ha] = [jnp.max(p, axis=-1, keepdims=True) for p in part_max]
        if do_b:
            row_sum[hb] = [jnp.sum(p, axis=-1, keepdims=True) for p in part_sum]
        if do_c:
            o = acc * (1.0 / l1)
            o = o * lax.rsqrt(jnp.mean(o * o, axis=-1, keepdims=True) + EPS) * sg_ref[...]
            o_ref[:, head_cols(hc)] = (o * (1.0 - lam_init)).astype(BF16)


def _attention(c_lambda, q, k, v, subln_g, lam_init, n_total, row0, prev_out):
    nb, lk, _ = k.shape
    lq = q.shape[0] // nb
    tq = _tile(lq, 256)
    nq = lq // tq
    rb = row0 // tq
    in_specs = [
        pl.BlockSpec((4, C_DK), lambda b, t: (0, 0)),
        pl.BlockSpec((tq, GROUP_W), lambda b, t: (b * nq + t, 0)),
        pl.BlockSpec((None, lk, GROUP_W), lambda b, t: (b, 0, 0)),
        pl.BlockSpec((None, lk, GROUP_W), lambda b, t: (b, 0, 0)),
        pl.BlockSpec((1, C_DV), lambda b, t: (0, 0)),
    ]
    args = [c_lambda, q, k, v, subln_g]
    aliases = {}
    if prev_out is not None:
        in_specs.append(pl.BlockSpec(memory_space=pl.ANY))
        args.append(prev_out)
        aliases = {5: 0}
    return pl.pallas_call(
        functools.partial(_attn_kernel, lam_init=lam_init),
        grid=(nb, nq),
        in_specs=in_specs,
        out_specs=pl.BlockSpec((tq, GROUP_W), lambda b, t: (rb + b * nq + t, 0)),
        out_shape=jax.ShapeDtypeStruct((n_total, GROUP_W), BF16),
        scratch_shapes=[pltpu.VMEM((3, 2, tq, lk), F32)],
        input_output_aliases=aliases,
        compiler_params=_params(("arbitrary", "arbitrary"), 56),
        name="diff_attn",
    )(*args)


def _dft_kernel(*refs, scale):
    dx_ref, wc_ref, c_ref, s_ref, dl_ref = refs[:5]
    o_ref, pq_ref = refs[-2], refs[-1]

    @pl.when(pl.program_id(1) == 0)
    def _():
        pq_ref[...] = _dot(dx_ref[...].astype(BF16), wc_ref[...]).astype(BF16)

    y = _dot(c_ref[...], pq_ref[:, 0:GROUP_W]) - _dot(s_ref[...], pq_ref[:, GROUP_W:2 * GROUP_W])
    o_ref[...] = _dot((y * scale).astype(BF16), dl_ref[...]).astype(BF16)


def _fourier(proj, wc, cl, sl, d_lin, nb, seq, n_total, row0, prev_out):
    tr = _tile(seq, 512)
    nr = seq // tr
    sb = row0 // seq
    rb = row0 // tr
    in_specs = [
        pl.BlockSpec((seq, GROUP_W), lambda b, r: (sb + b, N_GROUPS - 1)),
        pl.BlockSpec((GROUP_W, 2 * GROUP_W), lambda b, r: (0, 0)),
        pl.BlockSpec((tr, seq), lambda b, r: (r, 0)),
        pl.BlockSpec((tr, seq), lambda b, r: (r, 0)),
        pl.BlockSpec((GROUP_W, GROUP_W), lambda b, r: (0, 0)),
    ]
    args = [proj, wc, cl, sl, d_lin]
    aliases = {}
    if prev_out is not None:
        in_specs.append(pl.BlockSpec(memory_space=pl.ANY))
        args.append(prev_out)
        aliases = {5: 0}
    return pl.pallas_call(
        functools.partial(_dft_kernel, scale=1.0 / math.sqrt(seq * D_GW)),
        grid=(nb, nr),
        in_specs=in_specs,
        out_specs=pl.BlockSpec((tr, GROUP_W), lambda b, r: (rb + b * nr + r, 0)),
        out_shape=jax.ShapeDtypeStruct((n_total, GROUP_W), BF16),
        scratch_shapes=[pltpu.VMEM((seq, 2 * GROUP_W), BF16)],
        input_output_aliases=aliases,
        compiler_params=_params(("arbitrary", "arbitrary"), 56),
        name="fourier_mix",
    )(*args)


def _outproj_kernel(x_ref, mod_ref, a_ref, b_ref, c_ref, d_ref, w_ref, o_ref):
    y = _dot(a_ref[...], w_ref[0:GROUP_W, :])
    y += _dot(b_ref[...], w_ref[GROUP_W:2 * GROUP_W, :])
    y += _dot(c_ref[...], w_ref[2 * GROUP_W:3 * GROUP_W, :])
    y += _dot(d_ref[...], w_ref[3 * GROUP_W:4 * GROUP_W, :])
    o_ref[...] = x_ref[...] + mod_ref[5:6, :] * y


def _outproj(x, modl, oa, ob, oc, od, w_out, cond_of):
    n, d = x.shape
    tm = cond_of.tile(512)
    mix = pl.BlockSpec((tm, GROUP_W), lambda i: (i, 0))
    return pl.pallas_call(
        _outproj_kernel,
        grid=(n // tm,),
        in_specs=[
            pl.BlockSpec((tm, d), lambda i: (i, 0)),
            pl.BlockSpec((None, N_MOD, d), lambda i: (cond_of(i, tm), 0, 0)),
            mix, mix, mix, mix,
            pl.BlockSpec((4 * GROUP_W, d), lambda i: (0, 0)),
        ],
        out_specs=pl.BlockSpec((tm, d), lambda i: (i, 0)),
        out_shape=jax.ShapeDtypeStruct((n, d), F32),
        compiler_params=_params(("arbitrary",), 48),
        name="out_proj",
    )(x, modl, oa, ob, oc, od, w_out)


def _dft_tables(n, split=64):
    def trig(cols, stride):
        j = lax.broadcasted_iota(jnp.int32, (n, cols), 0)
        k = lax.broadcasted_iota(jnp.int32, (n, cols), 1) * stride
        ang = (2.0 * math.pi / n) * lax.rem(j * k, n).astype(F32)
        return jnp.cos(ang), jnp.sin(ang)

    if n % split or n <= split:
        return trig(n, 1)
    (ch, sh), (cl, sl) = trig(n // split, split), trig(split, 1)
    ch, sh, cl, sl = ch[:, :, None], sh[:, :, None], cl[:, None, :], sl[:, None, :]
    return (ch * cl - sh * sl).reshape(n, n), (sh * cl + ch * sl).reshape(n, n)


def _channel_dft_table():
    c, s = _dft_tables(D_GW)
    eye = jnp.eye(GROUP_W // D_GW, dtype=F32)
    return jnp.concatenate([jnp.kron(eye, c), jnp.kron(eye, s)], axis=1).astype(BF16)


def _rope_tables(n_tokens):
    rows = n_tokens // GRID_W
    row = jnp.repeat(jnp.arange(rows, dtype=F32), GRID_W)
    col = jnp.tile(jnp.arange(GRID_W, dtype=F32), rows)
    freqs = ROPE_THETA ** (-jnp.arange(ROPE_NF, dtype=F32) / ROPE_NF)
    ang_r = row[:, None] * freqs
    ang_c = col[:, None] * freqs

    def lanes(fr, fc, sign):
        grp = jnp.concatenate([sign * fr, fr, sign * fc, fc], axis=-1)
        return jnp.tile(grp, (1, GROUP_W // C_DK))

    cos = lanes(jnp.cos(ang_r), jnp.cos(ang_c), 1.0)
    sin = lanes(jnp.sin(ang_r), jnp.sin(ang_c), -1.0)
    return cos, sin


class _CondOf:
    def __init__(self, n_prompt, seq_s):
        self.n_prompt = n_prompt
        self.seq_s = seq_s

    def tile(self, target):
        return _tile(math.gcd(self.n_prompt, self.seq_s), target)

    def __call__(self, i, tm):
        r = i * tm
        return jnp.where(r < self.n_prompt, 0, 1 + (r - self.n_prompt) // self.seq_s)


def kernel(x_prompt, x_sample, c, cache_k, cache_v, c_ctx, w_mod, b_mod, norm_g, w_ff1_in, w_ff1_down, w_ff2_in, w_ff2_down, w_in, w_out, a_norm_g, a_ws, a_bs, b_conv_w, b_conv_b, b_ln_g, b_ln_b, b_pw, c_qnorm_g, c_knorm_g, c_lambda, c_subln_g, d_lin):
    nbp, seq_p, d = x_prompt.shape
    nbs, seq_s, _ = x_sample.shape
    depth = w_mod.shape[0]
    past = cache_k.shape[2]
    n_p, n_s = nbp * seq_p, nbs * seq_s
    n = n_p + n_s
    assert n_p % seq_s == 0 and seq_p % CHUNK == 0 and seq_s % CHUNK == 0
    assert 1 + nbs <= COND_PAD
    cond_of = _CondOf(n_p, seq_s)

    cond = jnp.zeros((COND_PAD, d), F32).at[0].set(c_ctx).at[1:1 + nbs].set(c)
    mods = _modulation(cond, w_mod, b_mod).reshape(depth, COND_PAD, N_MOD, d)

    wc = _channel_dft_table()
    tabs_p = [t.astype(BF16) for t in _dft_tables(seq_p)]
    tabs_s = [t.astype(BF16) for t in _dft_tables(seq_s)]
    rope = _rope_tables(seq_s)
    gidx = jnp.arange(GROUP_W) // C_DK
    gm = jnp.where(gidx[:, None] == gidx[None, :], 1.0 / C_DK, 0.0).astype(BF16)

    new_kv = None
    for l in range(depth):
        lam_init = 0.8 - 0.6 * math.exp(-0.3 * l)
        modl = mods[l]
        ng = norm_g[l]
        ff1 = (modl, ng[0:1], w_ff1_in[l].astype(BF16), w_ff1_down[l].astype(BF16), 0, cond_of)
        if l == 0:
            x = _ffn(x_prompt.reshape(n_p, d), *ff1, out_total=n)
            x = _ffn(x_sample.reshape(n_s, d), *ff1, tok_row0=n_p, out_total=n, out_row0=n_p, prev_out=x)
        else:
            x = _ffn(x, *ff1)

        proj = _inproj(x, modl, ng[1:2], w_in[l].astype(BF16), cond_of)

        bias_a = jnp.repeat(a_bs[l].T, CHUNK, axis=1)
        out_a = _mix_a(proj, a_norm_g[l][None], a_ws[l].astype(BF16), bias_a)

        out_b = _mix_b(proj, b_conv_w[l], b_conv_b[l][None], b_ln_g[l][None], b_ln_b[l][None],
                       b_pw[l].astype(BF16), n_p, seq_p, seq_s)

        gq = jnp.tile(c_qnorm_g[l], GROUP_W // C_DK)[None]
        gk = jnp.tile(c_knorm_g[l], GROUP_W // C_DK)[None]
        sg = c_subln_g[l][None]
        qp, kp, vp, *new_kv = _prep(proj, gq, gk, gm, 0, n_p, None, cache_out=(l, depth, seq_p, new_kv))
        qs, k_all, v_all = _prep(proj, gq, gk, gm, n_p, n_s, rope, past)
        k_all, v_all = _cache_fill(cache_k, cache_v, l, k_all, v_all)
        out_c = _attention(c_lambda[l], qp, kp.reshape(nbp, seq_p, GROUP_W), vp.reshape(nbp, seq_p, GROUP_W),
                           sg, lam_init, n, 0, None)
        out_c = _attention(c_lambda[l], qs, k_all, v_all, sg, lam_init, n, n_p, out_c)

        dl = d_lin[l].astype(BF16)
        out_d = _fourier(proj, wc, tabs_p[0], tabs_p[1], dl, nbp, seq_p, n, 0, None)
        out_d = _fourier(proj, wc, tabs_s[0], tabs_s[1], dl, nbs, seq_s, n, n_p, out_d)

        x = _outproj(x, modl, out_a, out_b, out_c, out_d, w_out[l].astype(BF16), cond_of)
        ff2 = (modl, ng[2:3], w_ff2_in[l].astype(BF16), w_ff2_down[l].astype(BF16), 6, cond_of)
        if l == depth - 1:
            y_p = _ffn(x, *ff2, nrows=n_p)
            y_s = _ffn(x, *ff2, in_row0=n_p, nrows=n_s, tok_row0=n_p)
        else:
            x = _ffn(x, *ff2)

    new_k, new_v = new_kv
    return (y_p.reshape(nbp, seq_p, d), y_s.reshape(nbs, seq_s, d), new_k, new_v)
```

```python
import functools
import math

import jax
import jax.numpy as jnp
from jax import lax
from jax.experimental import pallas as pl
from jax.experimental.pallas import tpu as pltpu

F32 = jnp.float32
BF16 = jnp.bfloat16

EPS = 1e-6
GROUP_W = 512
N_GROUPS = 8
CHUNK = 128
A_HEADS = 4
CONV_W = 31
CONV_HALO = 16
CONV_SUB = 32
SUBLANES = 8
C_HEADS = 4
C_DK = 64
C_DV = 128
Q_SCALE = C_DK ** -0.5 * math.log2(math.e)
ATTN_KEY_CHUNK = 512
ROPE_NF = 16
ROPE_THETA = 10000.0
GRID_W = 64
D_GW = 128
N_MOD = 9
COND_PAD = 8
MIB = 1024 * 1024
FFN_TM = 1024
FFN_TF = 512
FFN_NSPLIT = 4
FFN_VMEM_MIB = 60
NORM_ROWS = 16
NORM_UNROLL = 8


def _tile(n, target):
    t = target
    while n % t:
        t //= 2
    return t


def _params(sem, vmem_mib):
    return pltpu.CompilerParams(dimension_semantics=sem, vmem_limit_bytes=vmem_mib * MIB)


def _dot(a, b):
    return jnp.dot(a, b, preferred_element_type=F32)


def _mod_kernel(c_ref, w_ref, b_ref, o_ref):
    s = jax.nn.silu(c_ref[...]).astype(BF16)
    o_ref[...] = _dot(s, w_ref[...].astype(BF16)) + b_ref[...]


def _modulation(cond, w_mod, b_mod):
    depth, d, nd = w_mod.shape
    tn = _tile(nd, 1024)
    return pl.pallas_call(
        _mod_kernel,
        grid=(depth, nd // tn),
        in_specs=[
            pl.BlockSpec((COND_PAD, d), lambda l, j: (0, 0)),
            pl.BlockSpec((None, d, tn), lambda l, j: (l, 0, j)),
            pl.BlockSpec((None, 1, tn), lambda l, j: (l, 0, j)),
        ],
        out_specs=pl.BlockSpec((None, COND_PAD, tn), lambda l, j: (l, 0, j)),
        out_shape=jax.ShapeDtypeStruct((depth, COND_PAD, nd), F32),
        compiler_params=_params(("arbitrary", "arbitrary"), 40),
        name="modulation",
    )(cond, w_mod, b_mod.reshape(depth, 1, nd))


def _ada_norm_rows(x_ref, h_ref, g, shift, scale, copy_ref=None):
    gain = g * (1.0 + scale)

    def body(c, carry):
        rows = pl.ds(pl.multiple_of(c * NORM_ROWS, NORM_ROWS), NORM_ROWS)
        x = x_ref[rows, :]
        y = x * lax.rsqrt(jnp.mean(x * x, axis=-1, keepdims=True) + EPS)
        h_ref[rows, :] = (y * gain + shift).astype(BF16)
        if copy_ref is not None:
            copy_ref[rows, :] = x
        return carry

    lax.fori_loop(0, x_ref.shape[0] // NORM_ROWS, body, 0, unroll=NORM_UNROLL)


def _ffn_kernel(*refs, s, n_split):
    x_ref, mod_ref, g_ref, wu_ref, wd_ref = refs[:5]
    o_ref, h_ref = refs[-2:]

    @pl.when(pl.program_id(1) == 0)
    def _():
        _ada_norm_rows(x_ref, h_ref, g_ref[...], mod_ref[s:s + 1, :], mod_ref[s + 1:s + 2, :], copy_ref=o_ref)

    tf = wd_ref.shape[0]
    ag = _dot(h_ref[...], wu_ref[...])
    act = (jax.nn.silu(ag[:, tf:]) * ag[:, :tf]).astype(BF16)
    gate = 0.5 * mod_ref[s + 2:s + 3, :]
    cw = o_ref.shape[1] // n_split
    for c in range(n_split):
        cols = slice(c * cw, (c + 1) * cw)
        o_ref[:, cols] += gate[:, cols] * _dot(act, wd_ref[:, cols])


def _pack_ffn_up(w_up):
    depth, d, ff2 = w_up.shape
    nf = ff2 // 2 // FFN_TF
    w = w_up.reshape(depth, d, 2, nf, FFN_TF)
    return jnp.swapaxes(w, 2, 3).reshape(depth, d, ff2).astype(BF16)


def _ffn(x, modl, g, w_up, w_down, layer, s, cond_of, *, in_row0=0, nrows=None, tok_row0=0,
         out_total=None, out_row0=0, prev_out=None):
    d = x.shape[1]
    nrows = x.shape[0] if nrows is None else nrows
    out_total = nrows if out_total is None else out_total
    ff = w_down.shape[1]
    tm = cond_of.tile(FFN_TM)
    tf = FFN_TF
    nf = ff // tf
    ib, tb, ob = in_row0 // tm, tok_row0 // tm, out_row0 // tm
    in_specs = [
        pl.BlockSpec((tm, d), lambda i, f: (ib + i, 0)),
        pl.BlockSpec((None, N_MOD, d), lambda i, f: (cond_of(tb + i, tm), 0, 0)),
        pl.BlockSpec((1, d), lambda i, f: (0, 0)),
        pl.BlockSpec((None, d, 2 * tf), lambda i, f: (layer, 0, f)),
        pl.BlockSpec((None, tf, d), lambda i, f: (layer, f, 0)),
    ]
    args = [x, modl, g, w_up, w_down]
    aliases = {}
    if prev_out is not None:
        in_specs.append(pl.BlockSpec(memory_space=pl.ANY))
        args.append(prev_out)
        aliases = {5: 0}
    return pl.pallas_call(
        functools.partial(_ffn_kernel, s=s, n_split=FFN_NSPLIT),
        grid=(nrows // tm, nf),
        in_specs=in_specs,
        out_specs=pl.BlockSpec((tm, d), lambda i, f: (ob + i, 0)),
        out_shape=jax.ShapeDtypeStruct((out_total, d), F32),
        scratch_shapes=[pltpu.VMEM((tm, d), BF16)],
        input_output_aliases=aliases,
        compiler_params=_params(("arbitrary", "arbitrary"), FFN_VMEM_MIB),
        name="ffn",
    )(*args)


def _inproj_kernel(x0_ref, xn_ref, mod0_ref, modn_ref, g_ref, w_ref, o_ref, h_even_ref, h_odd_ref):
    i, j = pl.program_id(0), pl.program_id(1)
    g = g_ref[...]

    @pl.when((i == 0) & (j == 0))
    def _():
        _ada_norm_rows(x0_ref, h_even_ref, g, mod0_ref[3:4, :], mod0_ref[4:5, :])

    rows_per_step = xn_ref.shape[0]

    def step(h_cur_ref, h_next_ref):
        gain = g * (1.0 + modn_ref[4:5, :])
        shift = modn_ref[3:4, :]
        for c in range(rows_per_step // NORM_ROWS):
            x = xn_ref[c * NORM_ROWS:(c + 1) * NORM_ROWS, :]
            y = x * lax.rsqrt(jnp.mean(x * x, axis=-1, keepdims=True) + EPS)
            dst = pl.ds(pl.multiple_of(j * rows_per_step + c * NORM_ROWS, NORM_ROWS), NORM_ROWS)
            h_next_ref[dst, :] = (y * gain + shift).astype(BF16)
        o_ref[...] = _dot(h_cur_ref[...], w_ref[...])

    @pl.when(lax.rem(i, 2) == 0)
    def _():
        step(h_even_ref, h_odd_ref)

    @pl.when(lax.rem(i, 2) == 1)
    def _():
        step(h_odd_ref, h_even_ref)


def _inproj(x, modl, g, w_in, layer, cond_of):
    n, d = x.shape
    nc = w_in.shape[2]
    tm = cond_of.tile(1024)
    tn = _tile(nc, 1024)
    nj = nc // tn
    rows = tm // nj
    last = n // tm - 1
    nxt = lambda i: jnp.minimum(i + 1, last)
    return pl.pallas_call(
        _inproj_kernel,
        grid=(n // tm, nj),
        in_specs=[
            pl.BlockSpec((tm, d), lambda i, j: (0, 0)),
            pl.BlockSpec((rows, d), lambda i, j: (nxt(i) * nj + j, 0)),
            pl.BlockSpec((None, N_MOD, d), lambda i, j: (cond_of(0, tm), 0, 0)),
            pl.BlockSpec((None, N_MOD, d), lambda i, j: (cond_of(nxt(i), tm), 0, 0)),
            pl.BlockSpec((1, d), lambda i, j: (0, 0)),
            pl.BlockSpec((None, d, tn), lambda i, j: (layer, 0, j)),
        ],
        out_specs=pl.BlockSpec((tm, tn), lambda i, j: (i, j)),
        out_shape=jax.ShapeDtypeStruct((n, nc), F32),
        scratch_shapes=[pltpu.VMEM((tm, d), BF16), pltpu.VMEM((tm, d), BF16)],
        compiler_params=_params(("arbitrary", "arbitrary"), 48),
        name="in_proj",
    )(x, x, modl, modl, g, w_in)


def _mix_a_kernel(u_ref, v_ref, g_ref, ws_ref, bias_ref, o_ref, vv_ref, *, nchunk):
    v = jax.nn.gelu(v_ref[...])
    vv = v * lax.rsqrt(jnp.mean(v * v, axis=-1, keepdims=True) + EPS) * g_ref[...]
    vv_ref[...] = vv.astype(BF16)
    for n in range(nchunk):
        rows = slice(n * CHUNK, (n + 1) * CHUNK)
        for h in range(A_HEADS):
            cols = slice(h * CHUNK, (h + 1) * CHUNK)
            sp = _dot(ws_ref[h], vv_ref[rows, cols]) + bias_ref[:, cols]
            o_ref[rows, cols] = (jax.nn.gelu(u_ref[rows, cols]) * sp).astype(BF16)


def _mix_a(proj, a_norm_g, a_ws, bias):
    n = proj.shape[0]
    t = 512
    return pl.pallas_call(
        functools.partial(_mix_a_kernel, nchunk=t // CHUNK),
        grid=(n // t,),
        in_specs=[
            pl.BlockSpec((t, GROUP_W), lambda i: (i, 0)),
            pl.BlockSpec((t, GROUP_W), lambda i: (i, 1)),
            pl.BlockSpec((1, GROUP_W), lambda i: (0, 0)),
            pl.BlockSpec((A_HEADS, CHUNK, CHUNK), lambda i: (0, 0, 0)),
            pl.BlockSpec((CHUNK, GROUP_W), lambda i: (0, 0)),
        ],
        out_specs=pl.BlockSpec((t, GROUP_W), lambda i: (i, 0)),
        out_shape=jax.ShapeDtypeStruct((n, GROUP_W), BF16),
        scratch_shapes=[pltpu.VMEM((t, GROUP_W), BF16)],
        compiler_params=_params(("arbitrary",), 32),
        name="mix_a",
    )(proj, proj, a_norm_g, a_ws, bias)


def _mix_b_kernel(ap_ref, ac_ref, an_ref, gp_ref, gc_ref, gn_ref, w_ref, cb_ref, lg_ref, lb_ref,
                  pw_ref, o_ref, pad_ref, rot_ref, conv_ref, *, t, n_prompt, seq_p, seq_s):
    row0 = pl.program_id(0) * t
    in_prompt = row0 < n_prompt
    off = jnp.where(in_prompt, row0, row0 - n_prompt)
    seq = jnp.where(in_prompt, seq_p, seq_s)
    has_prev = lax.rem(off, seq) != 0
    has_next = lax.rem(off + t, seq) != 0

    def glu(a_ref, g_ref):
        return a_ref[...] * jax.nn.sigmoid(g_ref[...])

    pad_ref[0:CONV_HALO, :] = jnp.where(has_prev, glu(ap_ref, gp_ref), 0.0)
    pad_ref[CONV_HALO:CONV_HALO + t, :] = glu(ac_ref, gc_ref)
    pad_ref[CONV_HALO + t:2 * CONV_HALO + t, :] = jnp.where(has_next, glu(an_ref, gn_ref), 0.0)

    span = rot_ref.shape[1]
    for p in range(1, SUBLANES):
        rot_ref[p - 1, :, :] = pad_ref[p:p + span, :]

    base = CONV_HALO - CONV_W // 2
    for r in range(t // CONV_SUB):
        acc = jnp.broadcast_to(cb_ref[...], (CONV_SUB, GROUP_W))
        for j in range(CONV_W):
            p = (base + j) % SUBLANES
            start = r * CONV_SUB + base + j - p
            tap = pad_ref[start:start + CONV_SUB, :] if p == 0 else rot_ref[p - 1, start:start + CONV_SUB, :]
            acc = acc + tap * w_ref[j:j + 1, :]
        conv_ref[r * CONV_SUB:(r + 1) * CONV_SUB, :] = acc

    c = conv_ref[...]
    mu = jnp.mean(c, axis=-1, keepdims=True)
    dlt = c - mu
    y = dlt * lax.rsqrt(jnp.mean(dlt * dlt, axis=-1, keepdims=True) + EPS) * lg_ref[...] + lb_ref[...]
    o_ref[...] = _dot(jax.nn.silu(y).astype(BF16), pw_ref[...]).astype(BF16)


def _mix_b(proj, conv_w, conv_b, ln_g, ln_b, pw, n_prompt, seq_p, seq_s):
    n = proj.shape[0]
    t = _tile(math.gcd(seq_p, seq_s), 256)
    hb = t // CONV_HALO
    nhb = n // CONV_HALO
    cur = lambda c: pl.BlockSpec((t, GROUP_W), lambda i: (i, c))
    prev = lambda c: pl.BlockSpec((CONV_HALO, GROUP_W), lambda i: (jnp.maximum(i * hb - 1, 0), c))
    nxt = lambda c: pl.BlockSpec((CONV_HALO, GROUP_W), lambda i: (jnp.minimum((i + 1) * hb, nhb - 1), c))
    vec = pl.BlockSpec((1, GROUP_W), lambda i: (0, 0))
    return pl.pallas_call(
        functools.partial(_mix_b_kernel, t=t, n_prompt=n_prompt, seq_p=seq_p, seq_s=seq_s),
        grid=(n // t,),
        in_specs=[prev(2), cur(2), nxt(2), prev(3), cur(3), nxt(3),
                  pl.BlockSpec((CONV_W, GROUP_W), lambda i: (0, 0)), vec, vec, vec,
                  pl.BlockSpec((GROUP_W, GROUP_W), lambda i: (0, 0))],
        out_specs=pl.BlockSpec((t, GROUP_W), lambda i: (i, 0)),
        out_shape=jax.ShapeDtypeStruct((n, GROUP_W), BF16),
        scratch_shapes=[pltpu.VMEM((t + 2 * CONV_HALO, GROUP_W), F32),
                        pltpu.VMEM((SUBLANES - 1, t + 2 * CONV_HALO - SUBLANES, GROUP_W), F32),
                        pltpu.VMEM((t, GROUP_W), F32)],
        compiler_params=_params(("arbitrary",), 32),
        name="mix_b",
    )(proj, proj, proj, proj, proj, proj, conv_w, conv_b, ln_g, ln_b, pw)


def _group_mean_sq(x, gm_ref):
    s = x * x
    s_hi = s.astype(BF16)
    s_lo = (s - s_hi.astype(F32)).astype(BF16)
    return _dot(s_hi, gm_ref[...]) + _dot(s_lo, gm_ref[...])


def _swap16(x):
    n = x.shape[-1]
    lane = lax.broadcasted_iota(jnp.int32, x.shape, x.ndim - 1)
    first = lax.rem(lane, 32) < 16
    return jnp.where(first, pltpu.roll(x, n - 16, x.ndim - 1), pltpu.roll(x, 16, x.ndim - 1))


def _prep_prompt_kernel(*refs):
    q_ref, k_ref, v_ref, gq_ref, gk_ref, gm_ref = refs[:6]
    qo_ref, ko_ref, vo_ref, nk_ref, nv_ref = refs[-5:]
    q = q_ref[...]
    k = k_ref[...]
    v = v_ref[...]
    qn = q * lax.rsqrt(_group_mean_sq(q, gm_ref) + EPS) * gq_ref[...]
    kn = k * lax.rsqrt(_group_mean_sq(k, gm_ref) + EPS) * gk_ref[...]
    nseq, seq = nk_ref.shape[0], nk_ref.shape[1]
    for b in range(nseq):
        rows = slice(b * seq, (b + 1) * seq)
        for h in range(C_HEADS):
            cols = slice(h * C_DV, (h + 1) * C_DV)
            nk_ref[b, :, h, :] = kn[rows, cols]
            nv_ref[b, :, h, :] = v[rows, cols]
    qo_ref[...] = (qn * Q_SCALE).astype(BF16)
    ko_ref[...] = kn.astype(BF16)
    vo_ref[...] = v.astype(BF16)


def _prep_latent_kernel(q_ref, k_ref, v_ref, gq_ref, gk_ref, gm_ref, cos_ref, sin_ref, qo_ref, ko_ref, vo_ref):
    q = q_ref[...]
    k = k_ref[...]
    qn = q * lax.rsqrt(_group_mean_sq(q, gm_ref) + EPS) * gq_ref[...]
    kn = k * lax.rsqrt(_group_mean_sq(k, gm_ref) + EPS) * gk_ref[...]
    cos = cos_ref[...]
    sin = sin_ref[...]
    qr = qn * cos + _swap16(qn) * sin
    kr = kn * cos + _swap16(kn) * sin
    qo_ref[...] = (qr * Q_SCALE).astype(BF16)
    ko_ref[...] = kr.astype(BF16)
    vo_ref[...] = v_ref[...].astype(BF16)


def _prep(proj, gq, gk, gm, row0, nrows, rope, past=0, cache_out=None):
    if rope is None:
        t = _tile(nrows, max(512, cache_out[2]))
    else:
        t = _tile(math.gcd(nrows, past), 512)
    rb = row0 // t
    col = lambda c: pl.BlockSpec((t, GROUP_W), lambda i: (rb + i, c))
    vec = pl.BlockSpec((1, GROUP_W), lambda i: (0, 0))
    out = pl.BlockSpec((t, GROUP_W), lambda i: (i, 0))
    in_specs = [col(4), col(5), col(6), vec, vec, pl.BlockSpec((GROUP_W, GROUP_W), lambda i: (0, 0))]
    args = [proj, proj, proj, gq, gk, gm]
    shp = lambda dt: jax.ShapeDtypeStruct((nrows, GROUP_W), dt)
    aliases = {}
    if rope is None:
        layer, depth, seq, prev = cache_out
        nseq = t // seq
        cache = pl.BlockSpec((nseq, None, seq, C_HEADS, C_DV), lambda i: (i, layer, 0, 0, 0))
        cache_shape = jax.ShapeDtypeStruct((nrows // seq, depth, seq, C_HEADS, C_DV), F32)
        body, out_specs, out_shape = _prep_prompt_kernel, [out] * 3 + [cache] * 2, [shp(BF16)] * 3 + [cache_shape] * 2
        if prev is not None:
            in_specs += [pl.BlockSpec(memory_space=pl.ANY)] * 2
            args += list(prev)
            aliases = {6: 3, 7: 4}
    else:
        cos, sin = rope
        seq = cos.shape[0]
        npos = seq // t
        tab = pl.BlockSpec((t, GROUP_W), lambda i: (lax.rem(i, npos), 0))
        in_specs += [tab, tab]
        args += [cos, sin]
        kv = pl.BlockSpec((None, t, GROUP_W), lambda i: (i // npos, past // t + lax.rem(i, npos), 0))
        kv_shape = jax.ShapeDtypeStruct((nrows // seq, past + seq, GROUP_W), BF16)
        body, out_specs, out_shape = _prep_latent_kernel, [out, kv, kv], [shp(BF16), kv_shape, kv_shape]
    return pl.pallas_call(
        body, grid=(nrows // t,), in_specs=in_specs, out_specs=out_specs, out_shape=out_shape,
        input_output_aliases=aliases,
        compiler_params=_params(("arbitrary",), 32),
        name="qkv_prep_prompt" if rope is None else "qkv_prep_latent",
    )(*args)


def _cache_fill_kernel(ck_ref, cv_ref, k_in, v_in, ko_ref, vo_ref):
    for h in range(C_HEADS):
        cols = slice(h * C_DV, (h + 1) * C_DV)
        ko_ref[:, cols] = ck_ref[:, h, :].astype(BF16)
        vo_ref[:, cols] = cv_ref[:, h, :].astype(BF16)


def _cache_fill(cache_k, cache_v, layer, k_all, v_all):
    nbs, _, past, nh, dv = cache_k.shape
    cache = pl.BlockSpec((None, None, past, nh, dv), lambda b: (b, layer, 0, 0, 0))
    out = pl.BlockSpec((None, past, GROUP_W), lambda b: (b, 0, 0))
    hbm = pl.BlockSpec(memory_space=pl.ANY)
    return pl.pallas_call(
        _cache_fill_kernel, grid=(nbs,), in_specs=[cache, cache, hbm, hbm], out_specs=[out, out],
        out_shape=[jax.ShapeDtypeStruct(k_all.shape, BF16)] * 2,
        input_output_aliases={2: 0, 3: 1},
        compiler_params=_params(("arbitrary",), 32),
        name="cache_fill",
    )(cache_k, cache_v, k_all, v_all)


def _attn_kernel(*refs, lam_init):
    lam_ref, q_ref, k_ref, v_ref, sg_ref = refs[:5]
    lp = lam_ref[...]
    lam = (jnp.exp(jnp.sum(lp[0:1, :] * lp[1:2, :], axis=-1, keepdims=True))
           - jnp.exp(jnp.sum(lp[2:3, :] * lp[3:4, :], axis=-1, keepdims=True)) + lam_init)
    s_ref = refs[-1]
    o_ref = refs[-2]
    tq, lk = q_ref.shape[0], k_ref.shape[0]
    kc = _tile(lk, ATTN_KEY_CHUNK)
    lane = lax.broadcasted_iota(jnp.int32, (tq, C_DV), 1)
    first = lane < C_DK
    dims = (((1,), (1,)), ((), ()))
    head_cols = lambda h: slice(h * C_DV, (h + 1) * C_DV)

    def lane_groups(x):
        return [x[:, g * C_DV:(g + 1) * C_DV] for g in range(x.shape[1] // C_DV)]

    row_max, row_sum = {}, {}
    for t in range(C_HEADS + 2):
        ha, hb, hc = t, t - 1, t - 2
        do_a, do_b, do_c = ha < C_HEADS, 0 <= hb < C_HEADS, 0 <= hc
        if do_a:
            qh = q_ref[:, head_cols(ha)]
            zero = jnp.zeros_like(qh)
            q_maps = jnp.concatenate([jnp.where(first, qh, zero), jnp.where(first, zero, qh)], axis=0)
            part_max = [jnp.full((tq, C_DV), -jnp.inf, F32) for _ in range(2)]
        if do_b:
            part_sum = [jnp.zeros((tq, C_DV), F32) for _ in range(2)]
        if do_c:
            l1, l2 = row_sum[hc]
            r = lam * l1 * (1.0 / l2)
            acc = jnp.zeros((tq, C_DV), F32)
        for c in range(lk // kc):
            keys = slice(c * kc, (c + 1) * kc)
            if do_a:
                s_maps = lax.dot_general(q_maps, k_ref[keys, head_cols(ha)], dims, preferred_element_type=F32)
                for m in range(2):
                    s = s_maps[m * tq:(m + 1) * tq, :]
                    s_ref[ha % 3, m, :, keys] = s
                    part_max[m] = jnp.maximum(part_max[m], functools.reduce(jnp.maximum, lane_groups(s)))
            if do_b:
                for m in range(2):
                    e = jnp.exp2(s_ref[hb % 3, m, :, keys] - row_max[hb][m])
                    s_ref[hb % 3, m, :, keys] = e
                    part_sum[m] = part_sum[m] + functools.reduce(jnp.add, lane_groups(e))
            if do_c:
                a = (s_ref[hc % 3, 0, :, keys] - r * s_ref[hc % 3, 1, :, keys]).astype(BF16)
                acc = acc + _dot(a, v_ref[keys, head_cols(hc)])
        if do_a:
            row_max[ha] = [jnp.max(p, axis=-1, keepdims=True) for p in part_max]
        if do_b:
            row_sum[hb] = [jnp.sum(p, axis=-1, keepdims=True) for p in part_sum]
        if do_c:
            o = acc * (1.0 / l1)
            o = o * lax.rsqrt(jnp.mean(o * o, axis=-1, keepdims=True) + EPS) * sg_ref[...]
            o_ref[:, head_cols(hc)] = (o * (1.0 - lam_init)).astype(BF16)


def _attention(c_lambda, q, k, v, subln_g, lam_init, n_total, row0, prev_out):
    nb, lk, _ = k.shape
    lq = q.shape[0] // nb
    tq = _tile(lq, 256)
    nq = lq // tq
    rb = row0 // tq
    in_specs = [
        pl.BlockSpec((4, C_DK), lambda b, t: (0, 0)),
        pl.BlockSpec((tq, GROUP_W), lambda b, t: (b * nq + t, 0)),
        pl.BlockSpec((None, lk, GROUP_W), lambda b, t: (b, 0, 0)),
        pl.BlockSpec((None, lk, GROUP_W), lambda b, t: (b, 0, 0)),
        pl.BlockSpec((1, C_DV), lambda b, t: (0, 0)),
    ]
    args = [c_lambda, q, k, v, subln_g]
    aliases = {}
    if prev_out is not None:
        in_specs.append(pl.BlockSpec(memory_space=pl.ANY))
        args.append(prev_out)
        aliases = {5: 0}
    return pl.pallas_call(
        functools.partial(_attn_kernel, lam_init=lam_init),
        grid=(nb, nq),
        in_specs=in_specs,
        out_specs=pl.BlockSpec((tq, GROUP_W), lambda b, t: (rb + b * nq + t, 0)),
        out_shape=jax.ShapeDtypeStruct((n_total, GROUP_W), BF16),
        scratch_shapes=[pltpu.VMEM((3, 2, tq, lk), F32)],
        input_output_aliases=aliases,
        compiler_params=_params(("arbitrary", "arbitrary"), 56),
        name="diff_attn",
    )(*args)


def _dft_kernel(*refs, scale):
    dx_ref, wc_ref, c_ref, s_ref, dl_ref = refs[:5]
    o_ref, pq_ref = refs[-2], refs[-1]

    @pl.when(pl.program_id(1) == 0)
    def _():
        pq_ref[...] = _dot(dx_ref[...].astype(BF16), wc_ref[...]).astype(BF16)

    y = _dot(c_ref[...], pq_ref[:, 0:GROUP_W]) - _dot(s_ref[...], pq_ref[:, GROUP_W:2 * GROUP_W])
    o_ref[...] = _dot((y * scale).astype(BF16), dl_ref[...]).astype(BF16)


def _fourier(proj, wc, cl, sl, d_lin, nb, seq, n_total, row0, prev_out):
    tr = _tile(seq, 512)
    nr = seq // tr
    sb = row0 // seq
    rb = row0 // tr
    in_specs = [
        pl.BlockSpec((seq, GROUP_W), lambda b, r: (sb + b, N_GROUPS - 1)),
        pl.BlockSpec((GROUP_W, 2 * GROUP_W), lambda b, r: (0, 0)),
        pl.BlockSpec((tr, seq), lambda b, r: (r, 0)),
        pl.BlockSpec((tr, seq), lambda b, r: (r, 0)),
        pl.BlockSpec((GROUP_W, GROUP_W), lambda b, r: (0, 0)),
    ]
    args = [proj, wc, cl, sl, d_lin]
    aliases = {}
    if prev_out is not None:
        in_specs.append(pl.BlockSpec(memory_space=pl.ANY))
        args.append(prev_out)
        aliases = {5: 0}
    return pl.pallas_call(
        functools.partial(_dft_kernel, scale=1.0 / math.sqrt(seq * D_GW)),
        grid=(nb, nr),
        in_specs=in_specs,
        out_specs=pl.BlockSpec((tr, GROUP_W), lambda b, r: (rb + b * nr + r, 0)),
        out_shape=jax.ShapeDtypeStruct((n_total, GROUP_W), BF16),
        scratch_shapes=[pltpu.VMEM((seq, 2 * GROUP_W), BF16)],
        input_output_aliases=aliases,
        compiler_params=_params(("arbitrary", "arbitrary"), 56),
        name="fourier_mix",
    )(*args)


def _outproj_kernel(x_ref, mod_ref, a_ref, b_ref, c_ref, d_ref, w_ref, o_ref):
    y = _dot(a_ref[...], w_ref[0:GROUP_W, :])
    y += _dot(b_ref[...], w_ref[GROUP_W:2 * GROUP_W, :])
    y += _dot(c_ref[...], w_ref[2 * GROUP_W:3 * GROUP_W, :])
    y += _dot(d_ref[...], w_ref[3 * GROUP_W:4 * GROUP_W, :])
    o_ref[...] = x_ref[...] + mod_ref[5:6, :] * y


def _outproj(x, modl, oa, ob, oc, od, w_out, layer, cond_of):
    n, d = x.shape
    tm = cond_of.tile(512)
    mix = pl.BlockSpec((tm, GROUP_W), lambda i: (i, 0))
    return pl.pallas_call(
        _outproj_kernel,
        grid=(n // tm,),
        in_specs=[
            pl.BlockSpec((tm, d), lambda i: (i, 0)),
            pl.BlockSpec((None, N_MOD, d), lambda i: (cond_of(i, tm), 0, 0)),
            mix, mix, mix, mix,
            pl.BlockSpec((None, 4 * GROUP_W, d), lambda i: (layer, 0, 0)),
        ],
        out_specs=pl.BlockSpec((tm, d), lambda i: (i, 0)),
        out_shape=jax.ShapeDtypeStruct((n, d), F32),
        compiler_params=_params(("arbitrary",), 48),
        name="out_proj",
    )(x, modl, oa, ob, oc, od, w_out)


def _dft_tables(n, split=128):
    def trig(cols, stride):
        j = lax.broadcasted_iota(jnp.int32, (n, cols), 0)
        k = lax.broadcasted_iota(jnp.int32, (n, cols), 1) * stride
        ang = (2.0 * math.pi / n) * lax.rem(j * k, n).astype(F32)
        return jnp.cos(ang), jnp.sin(ang)

    if n % split or n <= split:
        return trig(n, 1)
    (ch, sh), (cl, sl) = trig(n // split, split), trig(split, 1)
    ch, sh, cl, sl = ch[:, :, None], sh[:, :, None], cl[:, None, :], sl[:, None, :]
    return (ch * cl - sh * sl).reshape(n, n), (sh * cl + ch * sl).reshape(n, n)


def _channel_dft_table():
    c, s = _dft_tables(D_GW)
    eye = jnp.eye(GROUP_W // D_GW, dtype=F32)
    return jnp.concatenate([jnp.kron(eye, c), jnp.kron(eye, s)], axis=1).astype(BF16)


def _rope_tables(n_tokens):
    rows = n_tokens // GRID_W
    row = jnp.repeat(jnp.arange(rows, dtype=F32), GRID_W)
    col = jnp.tile(jnp.arange(GRID_W, dtype=F32), rows)
    freqs = ROPE_THETA ** (-jnp.arange(ROPE_NF, dtype=F32) / ROPE_NF)
    ang_r = row[:, None] * freqs
    ang_c = col[:, None] * freqs

    def lanes(fr, fc, sign):
        grp = jnp.concatenate([sign * fr, fr, sign * fc, fc], axis=-1)
        return jnp.tile(grp, (1, GROUP_W // C_DK))

    cos = lanes(jnp.cos(ang_r), jnp.cos(ang_c), 1.0)
    sin = lanes(jnp.sin(ang_r), jnp.sin(ang_c), -1.0)
    return cos, sin


class _CondOf:
    def __init__(self, n_prompt, seq_s):
        self.n_prompt = n_prompt
        self.seq_s = seq_s

    def tile(self, target):
        return _tile(math.gcd(self.n_prompt, self.seq_s), target)

    def __call__(self, i, tm):
        r = i * tm
        return jnp.where(r < self.n_prompt, 0, 1 + (r - self.n_prompt) // self.seq_s)


def kernel(x_prompt, x_sample, c, cache_k, cache_v, c_ctx, w_mod, b_mod, norm_g, w_ff1_in, w_ff1_down, w_ff2_in, w_ff2_down, w_in, w_out, a_norm_g, a_ws, a_bs, b_conv_w, b_conv_b, b_ln_g, b_ln_b, b_pw, c_qnorm_g, c_knorm_g, c_lambda, c_subln_g, d_lin):
    nbp, seq_p, d = x_prompt.shape
    nbs, seq_s, _ = x_sample.shape
    depth = w_mod.shape[0]
    past = cache_k.shape[2]
    n_p, n_s = nbp * seq_p, nbs * seq_s
    n = n_p + n_s
    assert n_p % seq_s == 0 and seq_p % CHUNK == 0 and seq_s % CHUNK == 0
    assert 1 + nbs <= COND_PAD
    cond_of = _CondOf(n_p, seq_s)

    cond = jnp.zeros((COND_PAD, d), F32).at[0].set(c_ctx).at[1:1 + nbs].set(c)
    mods = _modulation(cond, w_mod, b_mod).reshape(depth, COND_PAD, N_MOD, d)

    wc = _channel_dft_table()
    tabs_p = [t.astype(BF16) for t in _dft_tables(seq_p)]
    tabs_s = [t.astype(BF16) for t in _dft_tables(seq_s)]
    rope = _rope_tables(seq_s)
    gidx = jnp.arange(GROUP_W) // C_DK
    gm = jnp.where(gidx[:, None] == gidx[None, :], 1.0 / C_DK, 0.0).astype(BF16)

    assert w_ff1_down.shape[1] % FFN_TF == 0
    wu1, wd1 = _pack_ffn_up(w_ff1_in), w_ff1_down.astype(BF16)
    wu2, wd2 = _pack_ffn_up(w_ff2_in), w_ff2_down.astype(BF16)
    w_in_b, w_out_b = w_in.astype(BF16), w_out.astype(BF16)

    new_kv = None
    for l in range(depth):
        lam_init = 0.8 - 0.6 * math.exp(-0.3 * l)
        modl = mods[l]
        ng = norm_g[l]
        ff1 = (modl, ng[0:1], wu1, wd1, l, 0, cond_of)
        if l == 0:
            x = _ffn(x_prompt.reshape(n_p, d), *ff1, out_total=n)
            x = _ffn(x_sample.reshape(n_s, d), *ff1, tok_row0=n_p, out_total=n, out_row0=n_p, prev_out=x)
        else:
            x = _ffn(x, *ff1)

        proj = _inproj(x, modl, ng[1:2], w_in_b, l, cond_of)

        bias_a = jnp.repeat(a_bs[l].T, CHUNK, axis=1)
        out_a = _mix_a(proj, a_norm_g[l][None], a_ws[l].astype(BF16), bias_a)

        out_b = _mix_b(proj, b_conv_w[l], b_conv_b[l][None], b_ln_g[l][None], b_ln_b[l][None],
                       b_pw[l].astype(BF16), n_p, seq_p, seq_s)

        gq = jnp.tile(c_qnorm_g[l], GROUP_W // C_DK)[None]
        gk = jnp.tile(c_knorm_g[l], GROUP_W // C_DK)[None]
        sg = c_subln_g[l][None]
        qp, kp, vp, *new_kv = _prep(proj, gq, gk, gm, 0, n_p, None, cache_out=(l, depth, seq_p, new_kv))
        qs, k_all, v_all = _prep(proj, gq, gk, gm, n_p, n_s, rope, past)
        k_all, v_all = _cache_fill(cache_k, cache_v, l, k_all, v_all)
        out_c = _attention(c_lambda[l], qp, kp.reshape(nbp, seq_p, GROUP_W), vp.reshape(nbp, seq_p, GROUP_W),
                           sg, lam_init, n, 0, None)
        out_c = _attention(c_lambda[l], qs, k_all, v_all, sg, lam_init, n, n_p, out_c)

        dl = d_lin[l].astype(BF16)
        out_d = _fourier(proj, wc, tabs_p[0], tabs_p[1], dl, nbp, seq_p, n, 0, None)
        out_d = _fourier(proj, wc, tabs_s[0], tabs_s[1], dl, nbs, seq_s, n, n_p, out_d)

        x = _outproj(x, modl, out_a, out_b, out_c, out_d, w_out_b, l, cond_of)
        ff2 = (modl, ng[2:3], wu2, wd2, l, 6, cond_of)
        if l == depth - 1:
            y_p = _ffn(x, *ff2, nrows=n_p)
            y_s = _ffn(x, *ff2, in_row0=n_p, nrows=n_s, tok_row0=n_p)
        else:
            x = _ffn(x, *ff2)

    new_k, new_v = new_kv
    return (y_p.reshape(nbp, seq_p, d), y_s.reshape(nbs, seq_s, d), new_k, new_v)
```

```python
import functools
import math

import jax
import jax.numpy as jnp
from jax import lax
from jax.experimental import pallas as pl
from jax.experimental.pallas import tpu as pltpu

F32 = jnp.float32
BF16 = jnp.bfloat16

EPS = 1e-6
GROUP_W = 512
N_GROUPS = 8
CHUNK = 128
A_HEADS = 4
CONV_W = 31
CONV_HALO = 16
CONV_SUB = 32
SUBLANES = 8
C_HEADS = 4
C_DK = 64
C_DV = 128
Q_SCALE = C_DK ** -0.5 * math.log2(math.e)
ATTN_KEY_CHUNK = 512
ROPE_NF = 16
ROPE_THETA = 10000.0
GRID_W = 64
D_GW = 128
N_MOD = 9
COND_PAD = 8
MIB = 1024 * 1024
FFN_TM = 1024
FFN_TF = 512
FFN_NSPLIT = 4
FFN_VMEM_MIB = 60
NORM_ROWS = 16
NORM_UNROLL = 8


def _tile(n, target):
    t = target
    while n % t:
        t //= 2
    return t


def _params(sem, vmem_mib):
    return pltpu.CompilerParams(dimension_semantics=sem, vmem_limit_bytes=vmem_mib * MIB)


def _dot(a, b):
    return jnp.dot(a, b, preferred_element_type=F32)


def _mod_kernel(c_ref, w_ref, b_ref, o_ref):
    s = jax.nn.silu(c_ref[...]).astype(BF16)
    o_ref[...] = _dot(s, w_ref[...].astype(BF16)) + b_ref[...]


def _modulation(cond, w_mod, b_mod):
    depth, d, nd = w_mod.shape
    tn = _tile(nd, 1024)
    return pl.pallas_call(
        _mod_kernel,
        grid=(depth, nd // tn),
        in_specs=[
            pl.BlockSpec((COND_PAD, d), lambda l, j: (0, 0)),
            pl.BlockSpec((None, d, tn), lambda l, j: (l, 0, j)),
            pl.BlockSpec((None, 1, tn), lambda l, j: (l, 0, j)),
        ],
        out_specs=pl.BlockSpec((None, COND_PAD, tn), lambda l, j: (l, 0, j)),
        out_shape=jax.ShapeDtypeStruct((depth, COND_PAD, nd), F32),
        compiler_params=_params(("arbitrary", "arbitrary"), 40),
        name="modulation",
    )(cond, w_mod, b_mod.reshape(depth, 1, nd))


def _ada_norm_rows(x_ref, h_ref, g, shift, scale, copy_ref=None):
    gain = g * (1.0 + scale)

    def body(c, carry):
        rows = pl.ds(pl.multiple_of(c * NORM_ROWS, NORM_ROWS), NORM_ROWS)
        x = x_ref[rows, :]
        y = x * lax.rsqrt(jnp.mean(x * x, axis=-1, keepdims=True) + EPS)
        h_ref[rows, :] = (y * gain + shift).astype(BF16)
        if copy_ref is not None:
            copy_ref[rows, :] = x
        return carry

    lax.fori_loop(0, x_ref.shape[0] // NORM_ROWS, body, 0, unroll=NORM_UNROLL)


def _ffn_kernel(*refs, s, n_split):
    x_ref, mod_ref, g_ref, wa_ref, wg_ref, wd_ref = refs[:6]
    o_ref, h_ref = refs[-2:]

    @pl.when(pl.program_id(1) == 0)
    def _():
        _ada_norm_rows(x_ref, h_ref, g_ref[...], mod_ref[s:s + 1, :], mod_ref[s + 1:s + 2, :], copy_ref=o_ref)

    h = h_ref[...]
    a = _dot(h, wa_ref[...])
    g = _dot(h, wg_ref[...])
    act = (jax.nn.silu(g) * a).astype(BF16)
    gate = 0.5 * mod_ref[s + 2:s + 3, :]
    cw = o_ref.shape[1] // n_split
    for c in range(n_split):
        cols = slice(c * cw, (c + 1) * cw)
        o_ref[:, cols] += gate[:, cols] * _dot(act, wd_ref[:, cols].astype(BF16))


def _ffn(x, modl, g, w_up, w_down, layer, s, cond_of, *, in_row0=0, nrows=None, tok_row0=0,
         out_total=None, out_row0=0, prev_out=None):
    d = x.shape[1]
    nrows = x.shape[0] if nrows is None else nrows
    out_total = nrows if out_total is None else out_total
    ff = w_down.shape[1]
    tm = cond_of.tile(FFN_TM)
    tf = FFN_TF
    nf = ff // tf
    ib, tb, ob = in_row0 // tm, tok_row0 // tm, out_row0 // tm
    in_specs = [
        pl.BlockSpec((tm, d), lambda i, f: (ib + i, 0)),
        pl.BlockSpec((None, N_MOD, d), lambda i, f: (cond_of(tb + i, tm), 0, 0)),
        pl.BlockSpec((1, d), lambda i, f: (0, 0)),
        pl.BlockSpec((None, d, tf), lambda i, f: (layer, 0, f)),
        pl.BlockSpec((None, d, tf), lambda i, f: (layer, 0, f + nf)),
        pl.BlockSpec((None, tf, d), lambda i, f: (layer, f, 0)),
    ]
    args = [x, modl, g, w_up, w_up, w_down]
    aliases = {}
    if prev_out is not None:
        in_specs.append(pl.BlockSpec(memory_space=pl.ANY))
        args.append(prev_out)
        aliases = {6: 0}
    return pl.pallas_call(
        functools.partial(_ffn_kernel, s=s, n_split=FFN_NSPLIT),
        grid=(nrows // tm, nf),
        in_specs=in_specs,
        out_specs=pl.BlockSpec((tm, d), lambda i, f: (ob + i, 0)),
        out_shape=jax.ShapeDtypeStruct((out_total, d), F32),
        scratch_shapes=[pltpu.VMEM((tm, d), BF16)],
        input_output_aliases=aliases,
        compiler_params=_params(("arbitrary", "arbitrary"), FFN_VMEM_MIB),
        name="ffn",
    )(*args)


def _inproj_kernel(x0_ref, xn_ref, mod0_ref, modn_ref, g_ref, w_ref, o_ref, h_even_ref, h_odd_ref):
    i, j = pl.program_id(0), pl.program_id(1)
    g = g_ref[...]

    @pl.when((i == 0) & (j == 0))
    def _():
        _ada_norm_rows(x0_ref, h_even_ref, g, mod0_ref[3:4, :], mod0_ref[4:5, :])

    rows_per_step = xn_ref.shape[0]

    def step(h_cur_ref, h_next_ref):
        gain = g * (1.0 + modn_ref[4:5, :])
        shift = modn_ref[3:4, :]
        for c in range(rows_per_step // NORM_ROWS):
            x = xn_ref[c * NORM_ROWS:(c + 1) * NORM_ROWS, :]
            y = x * lax.rsqrt(jnp.mean(x * x, axis=-1, keepdims=True) + EPS)
            dst = pl.ds(pl.multiple_of(j * rows_per_step + c * NORM_ROWS, NORM_ROWS), NORM_ROWS)
            h_next_ref[dst, :] = (y * gain + shift).astype(BF16)
        o_ref[...] = _dot(h_cur_ref[...], w_ref[...])

    @pl.when(lax.rem(i, 2) == 0)
    def _():
        step(h_even_ref, h_odd_ref)

    @pl.when(lax.rem(i, 2) == 1)
    def _():
        step(h_odd_ref, h_even_ref)


def _inproj(x, modl, g, w_in, layer, cond_of):
    n, d = x.shape
    nc = w_in.shape[2]
    tm = cond_of.tile(1024)
    tn = _tile(nc, 1024)
    nj = nc // tn
    rows = tm // nj
    last = n // tm - 1
    nxt = lambda i: jnp.minimum(i + 1, last)
    return pl.pallas_call(
        _inproj_kernel,
        grid=(n // tm, nj),
        in_specs=[
            pl.BlockSpec((tm, d), lambda i, j: (0, 0)),
            pl.BlockSpec((rows, d), lambda i, j: (nxt(i) * nj + j, 0)),
            pl.BlockSpec((None, N_MOD, d), lambda i, j: (cond_of(0, tm), 0, 0)),
            pl.BlockSpec((None, N_MOD, d), lambda i, j: (cond_of(nxt(i), tm), 0, 0)),
            pl.BlockSpec((1, d), lambda i, j: (0, 0)),
            pl.BlockSpec((None, d, tn), lambda i, j: (layer, 0, j)),
        ],
        out_specs=pl.BlockSpec((tm, tn), lambda i, j: (i, j)),
        out_shape=jax.ShapeDtypeStruct((n, nc), F32),
        scratch_shapes=[pltpu.VMEM((tm, d), BF16), pltpu.VMEM((tm, d), BF16)],
        compiler_params=_params(("arbitrary", "arbitrary"), 48),
        name="in_proj",
    )(x, x, modl, modl, g, w_in)


def _mix_a_kernel(u_ref, v_ref, g_ref, ws_ref, bias_ref, o_ref, vv_ref, *, nchunk):
    v = jax.nn.gelu(v_ref[...])
    vv = v * lax.rsqrt(jnp.mean(v * v, axis=-1, keepdims=True) + EPS) * g_ref[...]
    vv_ref[...] = vv.astype(BF16)
    for n in range(nchunk):
        rows = slice(n * CHUNK, (n + 1) * CHUNK)
        for h in range(A_HEADS):
            cols = slice(h * CHUNK, (h + 1) * CHUNK)
            sp = _dot(ws_ref[h], vv_ref[rows, cols]) + bias_ref[:, cols]
            o_ref[rows, cols] = (jax.nn.gelu(u_ref[rows, cols]) * sp).astype(BF16)


def _mix_a(proj, a_norm_g, a_ws, bias):
    n = proj.shape[0]
    t = 512
    return pl.pallas_call(
        functools.partial(_mix_a_kernel, nchunk=t // CHUNK),
        grid=(n // t,),
        in_specs=[
            pl.BlockSpec((t, GROUP_W), lambda i: (i, 0)),
            pl.BlockSpec((t, GROUP_W), lambda i: (i, 1)),
            pl.BlockSpec((1, GROUP_W), lambda i: (0, 0)),
            pl.BlockSpec((A_HEADS, CHUNK, CHUNK), lambda i: (0, 0, 0)),
            pl.BlockSpec((CHUNK, GROUP_W), lambda i: (0, 0)),
        ],
        out_specs=pl.BlockSpec((t, GROUP_W), lambda i: (i, 0)),
        out_shape=jax.ShapeDtypeStruct((n, GROUP_W), BF16),
        scratch_shapes=[pltpu.VMEM((t, GROUP_W), BF16)],
        compiler_params=_params(("arbitrary",), 32),
        name="mix_a",
    )(proj, proj, a_norm_g, a_ws, bias)


def _mix_b_kernel(ap_ref, ac_ref, an_ref, gp_ref, gc_ref, gn_ref, w_ref, cb_ref, lg_ref, lb_ref,
                  pw_ref, o_ref, pad_ref, rot_ref, conv_ref, *, t, n_prompt, seq_p, seq_s):
    row0 = pl.program_id(0) * t
    in_prompt = row0 < n_prompt
    off = jnp.where(in_prompt, row0, row0 - n_prompt)
    seq = jnp.where(in_prompt, seq_p, seq_s)
    has_prev = lax.rem(off, seq) != 0
    has_next = lax.rem(off + t, seq) != 0

    def glu(a_ref, g_ref):
        return a_ref[...] * jax.nn.sigmoid(g_ref[...])

    pad_ref[0:CONV_HALO, :] = jnp.where(has_prev, glu(ap_ref, gp_ref), 0.0)
    pad_ref[CONV_HALO:CONV_HALO + t, :] = glu(ac_ref, gc_ref)
    pad_ref[CONV_HALO + t:2 * CONV_HALO + t, :] = jnp.where(has_next, glu(an_ref, gn_ref), 0.0)

    span = rot_ref.shape[1]
    for p in range(1, SUBLANES):
        rot_ref[p - 1, :, :] = pad_ref[p:p + span, :]

    base = CONV_HALO - CONV_W // 2
    for r in range(t // CONV_SUB):
        acc = jnp.broadcast_to(cb_ref[...], (CONV_SUB, GROUP_W))
        for j in range(CONV_W):
            p = (base + j) % SUBLANES
            start = r * CONV_SUB + base + j - p
            tap = pad_ref[start:start + CONV_SUB, :] if p == 0 else rot_ref[p - 1, start:start + CONV_SUB, :]
            acc = acc + tap * w_ref[j:j + 1, :]
        conv_ref[r * CONV_SUB:(r + 1) * CONV_SUB, :] = acc

    c = conv_ref[...]
    mu = jnp.mean(c, axis=-1, keepdims=True)
    dlt = c - mu
    y = dlt * lax.rsqrt(jnp.mean(dlt * dlt, axis=-1, keepdims=True) + EPS) * lg_ref[...] + lb_ref[...]
    o_ref[...] = _dot(jax.nn.silu(y).astype(BF16), pw_ref[...]).astype(BF16)


def _mix_b(proj, conv_w, conv_b, ln_g, ln_b, pw, n_prompt, seq_p, seq_s):
    n = proj.shape[0]
    t = _tile(math.gcd(seq_p, seq_s), 256)
    hb = t // CONV_HALO
    nhb = n // CONV_HALO
    cur = lambda c: pl.BlockSpec((t, GROUP_W), lambda i: (i, c))
    prev = lambda c: pl.BlockSpec((CONV_HALO, GROUP_W), lambda i: (jnp.maximum(i * hb - 1, 0), c))
    nxt = lambda c: pl.BlockSpec((CONV_HALO, GROUP_W), lambda i: (jnp.minimum((i + 1) * hb, nhb - 1), c))
    vec = pl.BlockSpec((1, GROUP_W), lambda i: (0, 0))
    return pl.pallas_call(
        functools.partial(_mix_b_kernel, t=t, n_prompt=n_prompt, seq_p=seq_p, seq_s=seq_s),
        grid=(n // t,),
        in_specs=[prev(2), cur(2), nxt(2), prev(3), cur(3), nxt(3),
                  pl.BlockSpec((CONV_W, GROUP_W), lambda i: (0, 0)), vec, vec, vec,
                  pl.BlockSpec((GROUP_W, GROUP_W), lambda i: (0, 0))],
        out_specs=pl.BlockSpec((t, GROUP_W), lambda i: (i, 0)),
        out_shape=jax.ShapeDtypeStruct((n, GROUP_W), BF16),
        scratch_shapes=[pltpu.VMEM((t + 2 * CONV_HALO, GROUP_W), F32),
                        pltpu.VMEM((SUBLANES - 1, t + 2 * CONV_HALO - SUBLANES, GROUP_W), F32),
                        pltpu.VMEM((t, GROUP_W), F32)],
        compiler_params=_params(("arbitrary",), 32),
        name="mix_b",
    )(proj, proj, proj, proj, proj, proj, conv_w, conv_b, ln_g, ln_b, pw)


def _group_mean_sq(x, gm_ref):
    s = x * x
    s_hi = s.astype(BF16)
    s_lo = (s - s_hi.astype(F32)).astype(BF16)
    return _dot(s_hi, gm_ref[...]) + _dot(s_lo, gm_ref[...])


def _swap16(x):
    n = x.shape[-1]
    lane = lax.broadcasted_iota(jnp.int32, x.shape, x.ndim - 1)
    first = lax.rem(lane, 32) < 16
    return jnp.where(first, pltpu.roll(x, n - 16, x.ndim - 1), pltpu.roll(x, 16, x.ndim - 1))


def _prep_prompt_kernel(*refs):
    q_ref, k_ref, v_ref, gq_ref, gk_ref, gm_ref = refs[:6]
    qo_ref, ko_ref, vo_ref, nk_ref, nv_ref = refs[-5:]
    q = q_ref[...]
    k = k_ref[...]
    v = v_ref[...]
    qn = q * lax.rsqrt(_group_mean_sq(q, gm_ref) + EPS) * gq_ref[...]
    kn = k * lax.rsqrt(_group_mean_sq(k, gm_ref) + EPS) * gk_ref[...]
    nseq, seq = nk_ref.shape[0], nk_ref.shape[1]
    for b in range(nseq):
        rows = slice(b * seq, (b + 1) * seq)
        for h in range(C_HEADS):
            cols = slice(h * C_DV, (h + 1) * C_DV)
            nk_ref[b, :, h, :] = kn[rows, cols]
            nv_ref[b, :, h, :] = v[rows, cols]
    qo_ref[...] = (qn * Q_SCALE).astype(BF16)
    ko_ref[...] = kn.astype(BF16)
    vo_ref[...] = v.astype(BF16)


def _prep_latent_kernel(q_ref, k_ref, v_ref, gq_ref, gk_ref, gm_ref, cos_ref, sin_ref, qo_ref, ko_ref, vo_ref):
    q = q_ref[...]
    k = k_ref[...]
    qn = q * lax.rsqrt(_group_mean_sq(q, gm_ref) + EPS) * gq_ref[...]
    kn = k * lax.rsqrt(_group_mean_sq(k, gm_ref) + EPS) * gk_ref[...]
    cos = cos_ref[...]
    sin = sin_ref[...]
    qr = qn * cos + _swap16(qn) * sin
    kr = kn * cos + _swap16(kn) * sin
    qo_ref[...] = (qr * Q_SCALE).astype(BF16)
    ko_ref[...] = kr.astype(BF16)
    vo_ref[...] = v_ref[...].astype(BF16)


def _prep(proj, gq, gk, gm, row0, nrows, rope, past=0, cache_out=None):
    if rope is None:
        t = _tile(nrows, max(512, cache_out[2]))
    else:
        t = _tile(math.gcd(nrows, past), 512)
    rb = row0 // t
    col = lambda c: pl.BlockSpec((t, GROUP_W), lambda i: (rb + i, c))
    vec = pl.BlockSpec((1, GROUP_W), lambda i: (0, 0))
    out = pl.BlockSpec((t, GROUP_W), lambda i: (i, 0))
    in_specs = [col(4), col(5), col(6), vec, vec, pl.BlockSpec((GROUP_W, GROUP_W), lambda i: (0, 0))]
    args = [proj, proj, proj, gq, gk, gm]
    shp = lambda dt: jax.ShapeDtypeStruct((nrows, GROUP_W), dt)
    aliases = {}
    if rope is None:
        layer, depth, seq, prev = cache_out
        nseq = t // seq
        cache = pl.BlockSpec((nseq, None, seq, C_HEADS, C_DV), lambda i: (i, layer, 0, 0, 0))
        cache_shape = jax.ShapeDtypeStruct((nrows // seq, depth, seq, C_HEADS, C_DV), F32)
        body, out_specs, out_shape = _prep_prompt_kernel, [out] * 3 + [cache] * 2, [shp(BF16)] * 3 + [cache_shape] * 2
        if prev is not None:
            in_specs += [pl.BlockSpec(memory_space=pl.ANY)] * 2
            args += list(prev)
            aliases = {6: 3, 7: 4}
    else:
        cos, sin = rope
        seq = cos.shape[0]
        npos = seq // t
        tab = pl.BlockSpec((t, GROUP_W), lambda i: (lax.rem(i, npos), 0))
        in_specs += [tab, tab]
        args += [cos, sin]
        kv = pl.BlockSpec((None, t, GROUP_W), lambda i: (i // npos, past // t + lax.rem(i, npos), 0))
        kv_shape = jax.ShapeDtypeStruct((nrows // seq, past + seq, GROUP_W), BF16)
        body, out_specs, out_shape = _prep_latent_kernel, [out, kv, kv], [shp(BF16), kv_shape, kv_shape]
    return pl.pallas_call(
        body, grid=(nrows // t,), in_specs=in_specs, out_specs=out_specs, out_shape=out_shape,
        input_output_aliases=aliases,
        compiler_params=_params(("arbitrary",), 32),
        name="qkv_prep_prompt" if rope is None else "qkv_prep_latent",
    )(*args)


def _cache_fill_kernel(ck_ref, cv_ref, k_in, v_in, ko_ref, vo_ref):
    for h in range(C_HEADS):
        cols = slice(h * C_DV, (h + 1) * C_DV)
        ko_ref[:, cols] = ck_ref[:, h, :].astype(BF16)
        vo_ref[:, cols] = cv_ref[:, h, :].astype(BF16)


def _cache_fill(cache_k, cache_v, layer, k_all, v_all):
    nbs, _, past, nh, dv = cache_k.shape
    cache = pl.BlockSpec((None, None, past, nh, dv), lambda b: (b, layer, 0, 0, 0))
    out = pl.BlockSpec((None, past, GROUP_W), lambda b: (b, 0, 0))
    hbm = pl.BlockSpec(memory_space=pl.ANY)
    return pl.pallas_call(
        _cache_fill_kernel, grid=(nbs,), in_specs=[cache, cache, hbm, hbm], out_specs=[out, out],
        out_shape=[jax.ShapeDtypeStruct(k_all.shape, BF16)] * 2,
        input_output_aliases={2: 0, 3: 1},
        compiler_params=_params(("arbitrary",), 32),
        name="cache_fill",
    )(cache_k, cache_v, k_all, v_all)


def _attn_kernel(*refs, lam_init):
    lam_ref, q_ref, k_ref, v_ref, sg_ref = refs[:5]
    lp = lam_ref[...]
    lam = (jnp.exp(jnp.sum(lp[0:1, :] * lp[1:2, :], axis=-1, keepdims=True))
           - jnp.exp(jnp.sum(lp[2:3, :] * lp[3:4, :], axis=-1, keepdims=True)) + lam_init)
    s_ref = refs[-1]
    o_ref = refs[-2]
    tq, lk = q_ref.shape[0], k_ref.shape[0]
    kc = _tile(lk, ATTN_KEY_CHUNK)
    lane = lax.broadcasted_iota(jnp.int32, (tq, C_DV), 1)
    first = lane < C_DK
    dims = (((1,), (1,)), ((), ()))
    head_cols = lambda h: slice(h * C_DV, (h + 1) * C_DV)

    def lane_groups(x):
        return [x[:, g * C_DV:(g + 1) * C_DV] for g in range(x.shape[1] // C_DV)]

    row_max, row_sum = {}, {}
    for t in range(C_HEADS + 2):
        ha, hb, hc = t, t - 1, t - 2
        do_a, do_b, do_c = ha < C_HEADS, 0 <= hb < C_HEADS, 0 <= hc
        if do_a:
            qh = q_ref[:, head_cols(ha)]
            zero = jnp.zeros_like(qh)
            q_maps = jnp.concatenate([jnp.where(first, qh, zero), jnp.where(first, zero, qh)], axis=0)
            part_max = [jnp.full((tq, C_DV), -jnp.inf, F32) for _ in range(2)]
        if do_b:
            part_sum = [jnp.zeros((tq, C_DV), F32) for _ in range(2)]
        if do_c:
            l1, l2 = row_sum[hc]
            r = lam * l1 * (1.0 / l2)
            acc = jnp.zeros((tq, C_DV), F32)
        for c in range(lk // kc):
            keys = slice(c * kc, (c + 1) * kc)
            if do_a:
                s_maps = lax.dot_general(q_maps, k_ref[keys, head_cols(ha)], dims, preferred_element_type=F32)
                for m in range(2):
                    s = s_maps[m * tq:(m + 1) * tq, :]
                    s_ref[ha % 3, m, :, keys] = s
                    part_max[m] = jnp.maximum(part_max[m], functools.reduce(jnp.maximum, lane_groups(s)))
            if do_b:
                for m in range(2):
                    e = jnp.exp2(s_ref[hb % 3, m, :, keys] - row_max[hb][m])
                    s_ref[hb % 3, m, :, keys] = e
                    part_sum[m] = part_sum[m] + functools.reduce(jnp.add, lane_groups(e))
            if do_c:
                a = (s_ref[hc % 3, 0, :, keys] - r * s_ref[hc % 3, 1, :, keys]).astype(BF16)
                acc = acc + _dot(a, v_ref[keys, head_cols(hc)])
        if do_a:
            row_max[ha] = [jnp.max(p, axis=-1, keepdims=True) for p in part_max]
        if do_b:
            row_sum[hb] = [jnp.sum(p, axis=-1, keepdims=True) for p in part_sum]
        if do_c:
            o = acc * (1.0 / l1)
            o = o * lax.rsqrt(jnp.mean(o * o, axis=-1, keepdims=True) + EPS) * sg_ref[...]
            o_ref[:, head_cols(hc)] = (o * (1.0 - lam_init)).astype(BF16)


def _attention(c_lambda, q, k, v, subln_g, lam_init, n_total, row0, prev_out):
    nb, lk, _ = k.shape
    lq = q.shape[0] // nb
    tq = _tile(lq, 256)
    nq = lq // tq
    rb = row0 // tq
    in_specs = [
        pl.BlockSpec((4, C_DK), lambda b, t: (0, 0)),
        pl.BlockSpec((tq, GROUP_W), lambda b, t: (b * nq + t, 0)),
        pl.BlockSpec((None, lk, GROUP_W), lambda b, t: (b, 0, 0)),
        pl.BlockSpec((None, lk, GROUP_W), lambda b, t: (b, 0, 0)),
        pl.BlockSpec((1, C_DV), lambda b, t: (0, 0)),
    ]
    args = [c_lambda, q, k, v, subln_g]
    aliases = {}
    if prev_out is not None:
        in_specs.append(pl.BlockSpec(memory_space=pl.ANY))
        args.append(prev_out)
        aliases = {5: 0}
    return pl.pallas_call(
        functools.partial(_attn_kernel, lam_init=lam_init),
        grid=(nb, nq),
        in_specs=in_specs,
        out_specs=pl.BlockSpec((tq, GROUP_W), lambda b, t: (rb + b * nq + t, 0)),
        out_shape=jax.ShapeDtypeStruct((n_total, GROUP_W), BF16),
        scratch_shapes=[pltpu.VMEM((3, 2, tq, lk), F32)],
        input_output_aliases=aliases,
        compiler_params=_params(("arbitrary", "arbitrary"), 56),
        name="diff_attn",
    )(*args)


def _dft_kernel(*refs, scale):
    dx_ref, wc_ref, c_ref, s_ref, dl_ref = refs[:5]
    o_ref, pq_ref = refs[-2], refs[-1]

    @pl.when(pl.program_id(1) == 0)
    def _():
        pq_ref[...] = _dot(dx_ref[...].astype(BF16), wc_ref[...]).astype(BF16)

    y = _dot(c_ref[...], pq_ref[:, 0:GROUP_W]) - _dot(s_ref[...], pq_ref[:, GROUP_W:2 * GROUP_W])
    o_ref[...] = _dot((y * scale).astype(BF16), dl_ref[...]).astype(BF16)


def _fourier(proj, wc, cl, sl, d_lin, nb, seq, n_total, row0, prev_out):
    tr = _tile(seq, 512)
    nr = seq // tr
    sb = row0 // seq
    rb = row0 // tr
    in_specs = [
        pl.BlockSpec((seq, GROUP_W), lambda b, r: (sb + b, N_GROUPS - 1)),
        pl.BlockSpec((GROUP_W, 2 * GROUP_W), lambda b, r: (0, 0)),
        pl.BlockSpec((tr, seq), lambda b, r: (r, 0)),
        pl.BlockSpec((tr, seq), lambda b, r: (r, 0)),
        pl.BlockSpec((GROUP_W, GROUP_W), lambda b, r: (0, 0)),
    ]
    args = [proj, wc, cl, sl, d_lin]
    aliases = {}
    if prev_out is not None:
        in_specs.append(pl.BlockSpec(memory_space=pl.ANY))
        args.append(prev_out)
        aliases = {5: 0}
    return pl.pallas_call(
        functools.partial(_dft_kernel, scale=1.0 / math.sqrt(seq * D_GW)),
        grid=(nb, nr),
        in_specs=in_specs,
        out_specs=pl.BlockSpec((tr, GROUP_W), lambda b, r: (rb + b * nr + r, 0)),
        out_shape=jax.ShapeDtypeStruct((n_total, GROUP_W), BF16),
        scratch_shapes=[pltpu.VMEM((seq, 2 * GROUP_W), BF16)],
        input_output_aliases=aliases,
        compiler_params=_params(("arbitrary", "arbitrary"), 56),
        name="fourier_mix",
    )(*args)


def _outproj_kernel(x_ref, mod_ref, a_ref, b_ref, c_ref, d_ref, w_ref, o_ref):
    y = _dot(a_ref[...], w_ref[0:GROUP_W, :])
    y += _dot(b_ref[...], w_ref[GROUP_W:2 * GROUP_W, :])
    y += _dot(c_ref[...], w_ref[2 * GROUP_W:3 * GROUP_W, :])
    y += _dot(d_ref[...], w_ref[3 * GROUP_W:4 * GROUP_W, :])
    o_ref[...] = x_ref[...] + mod_ref[5:6, :] * y


def _outproj(x, modl, oa, ob, oc, od, w_out, layer, cond_of):
    n, d = x.shape
    tm = cond_of.tile(512)
    mix = pl.BlockSpec((tm, GROUP_W), lambda i: (i, 0))
    return pl.pallas_call(
        _outproj_kernel,
        grid=(n // tm,),
        in_specs=[
            pl.BlockSpec((tm, d), lambda i: (i, 0)),
            pl.BlockSpec((None, N_MOD, d), lambda i: (cond_of(i, tm), 0, 0)),
            mix, mix, mix, mix,
            pl.BlockSpec((None, 4 * GROUP_W, d), lambda i: (layer, 0, 0)),
        ],
        out_specs=pl.BlockSpec((tm, d), lambda i: (i, 0)),
        out_shape=jax.ShapeDtypeStruct((n, d), F32),
        compiler_params=_params(("arbitrary",), 48),
        name="out_proj",
    )(x, modl, oa, ob, oc, od, w_out)


def _dft_tables(n, split=128):
    def trig(cols, stride):
        j = lax.broadcasted_iota(jnp.int32, (n, cols), 0)
        k = lax.broadcasted_iota(jnp.int32, (n, cols), 1) * stride
        ang = (2.0 * math.pi / n) * lax.rem(j * k, n).astype(F32)
        return jnp.cos(ang), jnp.sin(ang)

    if n % split or n <= split:
        return trig(n, 1)
    (ch, sh), (cl, sl) = trig(n // split, split), trig(split, 1)
    ch, sh, cl, sl = ch[:, :, None], sh[:, :, None], cl[:, None, :], sl[:, None, :]
    return (ch * cl - sh * sl).reshape(n, n), (sh * cl + ch * sl).reshape(n, n)


def _channel_dft_table():
    c, s = _dft_tables(D_GW)
    eye = jnp.eye(GROUP_W // D_GW, dtype=F32)
    return jnp.concatenate([jnp.kron(eye, c), jnp.kron(eye, s)], axis=1).astype(BF16)


def _rope_tables(n_tokens):
    rows = n_tokens // GRID_W
    row = jnp.repeat(jnp.arange(rows, dtype=F32), GRID_W)
    col = jnp.tile(jnp.arange(GRID_W, dtype=F32), rows)
    freqs = ROPE_THETA ** (-jnp.arange(ROPE_NF, dtype=F32) / ROPE_NF)
    ang_r = row[:, None] * freqs
    ang_c = col[:, None] * freqs

    def lanes(fr, fc, sign):
        grp = jnp.concatenate([sign * fr, fr, sign * fc, fc], axis=-1)
        return jnp.tile(grp, (1, GROUP_W // C_DK))

    cos = lanes(jnp.cos(ang_r), jnp.cos(ang_c), 1.0)
    sin = lanes(jnp.sin(ang_r), jnp.sin(ang_c), -1.0)
    return cos, sin


class _CondOf:
    def __init__(self, n_prompt, seq_s):
        self.n_prompt = n_prompt
        self.seq_s = seq_s

    def tile(self, target):
        return _tile(math.gcd(self.n_prompt, self.seq_s), target)

    def __call__(self, i, tm):
        r = i * tm
        return jnp.where(r < self.n_prompt, 0, 1 + (r - self.n_prompt) // self.seq_s)


def kernel(x_prompt, x_sample, c, cache_k, cache_v, c_ctx, w_mod, b_mod, norm_g, w_ff1_in, w_ff1_down, w_ff2_in, w_ff2_down, w_in, w_out, a_norm_g, a_ws, a_bs, b_conv_w, b_conv_b, b_ln_g, b_ln_b, b_pw, c_qnorm_g, c_knorm_g, c_lambda, c_subln_g, d_lin):
    nbp, seq_p, d = x_prompt.shape
    nbs, seq_s, _ = x_sample.shape
    depth = w_mod.shape[0]
    past = cache_k.shape[2]
    n_p, n_s = nbp * seq_p, nbs * seq_s
    n = n_p + n_s
    assert n_p % seq_s == 0 and seq_p % CHUNK == 0 and seq_s % CHUNK == 0
    assert 1 + nbs <= COND_PAD
    cond_of = _CondOf(n_p, seq_s)

    cond = jnp.zeros((COND_PAD, d), F32).at[0].set(c_ctx).at[1:1 + nbs].set(c)
    mods = _modulation(cond, w_mod, b_mod).reshape(depth, COND_PAD, N_MOD, d)

    wc = _channel_dft_table()
    tabs_p = [t.astype(BF16) for t in _dft_tables(seq_p)]
    tabs_s = [t.astype(BF16) for t in _dft_tables(seq_s)]
    rope = _rope_tables(seq_s)
    gidx = jnp.arange(GROUP_W) // C_DK
    gm = jnp.where(gidx[:, None] == gidx[None, :], 1.0 / C_DK, 0.0).astype(BF16)

    assert w_ff1_down.shape[1] % FFN_TF == 0
    wu1, wd1 = w_ff1_in.astype(BF16), w_ff1_down
    wu2, wd2 = w_ff2_in.astype(BF16), w_ff2_down
    w_in_b, w_out_b = w_in.astype(BF16), w_out.astype(BF16)

    new_kv = None
    for l in range(depth):
        lam_init = 0.8 - 0.6 * math.exp(-0.3 * l)
        modl = mods[l]
        ng = norm_g[l]
        ff1 = (modl, ng[0:1], wu1, wd1, l, 0, cond_of)
        if l == 0:
            x = _ffn(x_prompt.reshape(n_p, d), *ff1, out_total=n)
            x = _ffn(x_sample.reshape(n_s, d), *ff1, tok_row0=n_p, out_total=n, out_row0=n_p, prev_out=x)
        else:
            x = _ffn(x, *ff1)

        proj = _inproj(x, modl, ng[1:2], w_in_b, l, cond_of)

        bias_a = jnp.repeat(a_bs[l].T, CHUNK, axis=1)
        out_a = _mix_a(proj, a_norm_g[l][None], a_ws[l].astype(BF16), bias_a)

        out_b = _mix_b(proj, b_conv_w[l], b_conv_b[l][None], b_ln_g[l][None], b_ln_b[l][None],
                       b_pw[l].astype(BF16), n_p, seq_p, seq_s)

        gq = jnp.tile(c_qnorm_g[l], GROUP_W // C_DK)[None]
        gk = jnp.tile(c_knorm_g[l], GROUP_W // C_DK)[None]
        sg = c_subln_g[l][None]
        qp, kp, vp, *new_kv = _prep(proj, gq, gk, gm, 0, n_p, None, cache_out=(l, depth, seq_p, new_kv))
        qs, k_all, v_all = _prep(proj, gq, gk, gm, n_p, n_s, rope, past)
        k_all, v_all = _cache_fill(cache_k, cache_v, l, k_all, v_all)
        out_c = _attention(c_lambda[l], qp, kp.reshape(nbp, seq_p, GROUP_W), vp.reshape(nbp, seq_p, GROUP_W),
                           sg, lam_init, n, 0, None)
        out_c = _attention(c_lambda[l], qs, k_all, v_all, sg, lam_init, n, n_p, out_c)

        dl = d_lin[l].astype(BF16)
        out_d = _fourier(proj, wc, tabs_p[0], tabs_p[1], dl, nbp, seq_p, n, 0, None)
        out_d = _fourier(proj, wc, tabs_s[0], tabs_s[1], dl, nbs, seq_s, n, n_p, out_d)

        x = _outproj(x, modl, out_a, out_b, out_c, out_d, w_out_b, l, cond_of)
        ff2 = (modl, ng[2:3], wu2, wd2, l, 6, cond_of)
        if l == depth - 1:
            y_p = _ffn(x, *ff2, nrows=n_p)
            y_s = _ffn(x, *ff2, in_row0=n_p, nrows=n_s, tok_row0=n_p)
        else:
            x = _ffn(x, *ff2)

    new_k, new_v = new_kv
    return (y_p.reshape(nbp, seq_p, d), y_s.reshape(nbs, seq_s, d), new_k, new_v)
```

```python
import functools
import math

import jax
import jax.numpy as jnp
from jax import lax
from jax.experimental import pallas as pl
from jax.experimental.pallas import tpu as pltpu

F32 = jnp.float32
BF16 = jnp.bfloat16

EPS = 1e-6
GROUP_W = 512
N_GROUPS = 8
CHUNK = 128
A_HEADS = 4
CONV_W = 31
CONV_HALO = 16
CONV_SUB = 32
SUBLANES = 8
C_HEADS = 4
C_DK = 64
C_DV = 128
Q_SCALE = C_DK ** -0.5 * math.log2(math.e)
ATTN_KEY_CHUNK = 512
ROPE_NF = 16
ROPE_THETA = 10000.0
GRID_W = 64
D_GW = 128
N_MOD = 9
COND_PAD = 8
MIB = 1024 * 1024
FFN_TM = 1024
FFN_TF = 512
FFN_NSPLIT = 4
FFN_VMEM_MIB = 60
NORM_ROWS = 16
NORM_UNROLL = 8


def _tile(n, target):
    t = target
    while n % t:
        t //= 2
    return t


def _params(sem, vmem_mib):
    return pltpu.CompilerParams(dimension_semantics=sem, vmem_limit_bytes=vmem_mib * MIB)


def _dot(a, b):
    return jnp.dot(a, b, preferred_element_type=F32)


def _mod_kernel(c_ref, w_ref, b_ref, o_ref):
    s = jax.nn.silu(c_ref[...]).astype(BF16)
    o_ref[...] = _dot(s, w_ref[...].astype(BF16)) + b_ref[...]


def _modulation(cond, w_mod, b_mod):
    depth, d, nd = w_mod.shape
    tn = _tile(nd, 1024)
    return pl.pallas_call(
        _mod_kernel,
        grid=(depth, nd // tn),
        in_specs=[
            pl.BlockSpec((COND_PAD, d), lambda l, j: (0, 0)),
            pl.BlockSpec((None, d, tn), lambda l, j: (l, 0, j)),
            pl.BlockSpec((None, 1, tn), lambda l, j: (l, 0, j)),
        ],
        out_specs=pl.BlockSpec((None, COND_PAD, tn), lambda l, j: (l, 0, j)),
        out_shape=jax.ShapeDtypeStruct((depth, COND_PAD, nd), F32),
        compiler_params=_params(("arbitrary", "arbitrary"), 40),
        name="modulation",
    )(cond, w_mod, b_mod.reshape(depth, 1, nd))


def _ada_norm_rows(x_ref, h_ref, g, shift, scale, copy_ref=None):
    gain = g * (1.0 + scale)

    def body(c, carry):
        rows = pl.ds(pl.multiple_of(c * NORM_ROWS, NORM_ROWS), NORM_ROWS)
        x = x_ref[rows, :]
        y = x * lax.rsqrt(jnp.mean(x * x, axis=-1, keepdims=True) + EPS)
        h_ref[rows, :] = (y * gain + shift).astype(BF16)
        if copy_ref is not None:
            copy_ref[rows, :] = x
        return carry

    lax.fori_loop(0, x_ref.shape[0] // NORM_ROWS, body, 0, unroll=NORM_UNROLL)


def _ffn_kernel(*refs, s, n_split, first_tile):
    x_hbm, mod_ref, g_ref, wa_ref, wg_ref, wd_ref = refs[:6]
    o_ref, h_ref, x_ref, x_sem = refs[-4:]
    i, f = pl.program_id(0), pl.program_id(1)
    tm = x_ref.shape[0]

    def x_copy(tile):
        rows = pl.ds(pl.multiple_of((first_tile + tile) * tm, tm), tm)
        return pltpu.make_async_copy(x_hbm.at[rows, :], x_ref, x_sem)

    @pl.when((i == 0) & (f == 0))
    def _():
        x_copy(0).start()

    @pl.when(f == 0)
    def _():
        x_copy(i).wait()
        _ada_norm_rows(x_ref, h_ref, g_ref[...], mod_ref[s:s + 1, :], mod_ref[s + 1:s + 2, :], copy_ref=o_ref)

    @pl.when((f == 1) & (i + 1 < pl.num_programs(0)))
    def _():
        x_copy(i + 1).start()

    h = h_ref[...]
    a = _dot(h, wa_ref[...].astype(BF16))
    g = _dot(h, wg_ref[...].astype(BF16))
    act = (jax.nn.silu(g) * a).astype(BF16)
    gate = 0.5 * mod_ref[s + 2:s + 3, :]
    cw = o_ref.shape[1] // n_split
    for c in range(n_split):
        cols = slice(c * cw, (c + 1) * cw)
        o_ref[:, cols] += gate[:, cols] * _dot(act, wd_ref[:, cols].astype(BF16))


def _ffn(x, modl, g, w_up, w_down, layer, s, cond_of, *, in_row0=0, nrows=None, tok_row0=0,
         out_total=None, out_row0=0, prev_out=None):
    d = x.shape[1]
    nrows = x.shape[0] if nrows is None else nrows
    out_total = nrows if out_total is None else out_total
    ff = w_down.shape[1]
    tm = cond_of.tile(FFN_TM)
    tf = FFN_TF
    nf = ff // tf
    assert nf >= 2
    ib, tb, ob = in_row0 // tm, tok_row0 // tm, out_row0 // tm
    in_specs = [
        pl.BlockSpec(memory_space=pl.ANY),
        pl.BlockSpec((None, N_MOD, d), lambda i, f: (cond_of(tb + i, tm), 0, 0)),
        pl.BlockSpec((1, d), lambda i, f: (0, 0)),
        pl.BlockSpec((None, d, tf), lambda i, f: (layer, 0, f)),
        pl.BlockSpec((None, d, tf), lambda i, f: (layer, 0, f + nf)),
        pl.BlockSpec((None, tf, d), lambda i, f: (layer, f, 0)),
    ]
    args = [x, modl, g, w_up, w_up, w_down]
    aliases = {}
    if prev_out is not None:
        in_specs.append(pl.BlockSpec(memory_space=pl.ANY))
        args.append(prev_out)
        aliases = {6: 0}
    return pl.pallas_call(
        functools.partial(_ffn_kernel, s=s, n_split=FFN_NSPLIT, first_tile=ib),
        grid=(nrows // tm, nf),
        in_specs=in_specs,
        out_specs=pl.BlockSpec((tm, d), lambda i, f: (ob + i, 0)),
        out_shape=jax.ShapeDtypeStruct((out_total, d), F32),
        scratch_shapes=[pltpu.VMEM((tm, d), BF16), pltpu.VMEM((tm, d), F32), pltpu.SemaphoreType.DMA(())],
        input_output_aliases=aliases,
        compiler_params=_params(("arbitrary", "arbitrary"), FFN_VMEM_MIB),
        name="ffn",
    )(*args)


def _inproj_kernel(x0_ref, xn_ref, mod0_ref, modn_ref, g_ref, w_ref, o_ref, h_even_ref, h_odd_ref):
    i, j = pl.program_id(0), pl.program_id(1)
    g = g_ref[...]

    @pl.when((i == 0) & (j == 0))
    def _():
        _ada_norm_rows(x0_ref, h_even_ref, g, mod0_ref[3:4, :], mod0_ref[4:5, :])

    rows_per_step = xn_ref.shape[0]

    def step(h_cur_ref, h_next_ref):
        gain = g * (1.0 + modn_ref[4:5, :])
        shift = modn_ref[3:4, :]
        for c in range(rows_per_step // NORM_ROWS):
            x = xn_ref[c * NORM_ROWS:(c + 1) * NORM_ROWS, :]
            y = x * lax.rsqrt(jnp.mean(x * x, axis=-1, keepdims=True) + EPS)
            dst = pl.ds(pl.multiple_of(j * rows_per_step + c * NORM_ROWS, NORM_ROWS), NORM_ROWS)
            h_next_ref[dst, :] = (y * gain + shift).astype(BF16)
        o_ref[...] = _dot(h_cur_ref[...], w_ref[...])

    @pl.when(lax.rem(i, 2) == 0)
    def _():
        step(h_even_ref, h_odd_ref)

    @pl.when(lax.rem(i, 2) == 1)
    def _():
        step(h_odd_ref, h_even_ref)


def _inproj(x, modl, g, w_in, layer, cond_of):
    n, d = x.shape
    nc = w_in.shape[2]
    tm = cond_of.tile(1024)
    tn = _tile(nc, 1024)
    nj = nc // tn
    rows = tm // nj
    last = n // tm - 1
    nxt = lambda i: jnp.minimum(i + 1, last)
    return pl.pallas_call(
        _inproj_kernel,
        grid=(n // tm, nj),
        in_specs=[
            pl.BlockSpec((tm, d), lambda i, j: (0, 0)),
            pl.BlockSpec((rows, d), lambda i, j: (nxt(i) * nj + j, 0)),
            pl.BlockSpec((None, N_MOD, d), lambda i, j: (cond_of(0, tm), 0, 0)),
            pl.BlockSpec((None, N_MOD, d), lambda i, j: (cond_of(nxt(i), tm), 0, 0)),
            pl.BlockSpec((1, d), lambda i, j: (0, 0)),
            pl.BlockSpec((None, d, tn), lambda i, j: (layer, 0, j)),
        ],
        out_specs=pl.BlockSpec((tm, tn), lambda i, j: (i, j)),
        out_shape=jax.ShapeDtypeStruct((n, nc), F32),
        scratch_shapes=[pltpu.VMEM((tm, d), BF16), pltpu.VMEM((tm, d), BF16)],
        compiler_params=_params(("arbitrary", "arbitrary"), 48),
        name="in_proj",
    )(x, x, modl, modl, g, w_in)


def _mix_a_kernel(u_ref, v_ref, g_ref, ws_ref, bias_ref, o_ref, vv_ref, *, nchunk):
    v = jax.nn.gelu(v_ref[...])
    vv = v * lax.rsqrt(jnp.mean(v * v, axis=-1, keepdims=True) + EPS) * g_ref[...]
    vv_ref[...] = vv.astype(BF16)
    for n in range(nchunk):
        rows = slice(n * CHUNK, (n + 1) * CHUNK)
        for h in range(A_HEADS):
            cols = slice(h * CHUNK, (h + 1) * CHUNK)
            sp = _dot(ws_ref[h], vv_ref[rows, cols]) + bias_ref[:, cols]
            o_ref[rows, cols] = (jax.nn.gelu(u_ref[rows, cols]) * sp).astype(BF16)


def _mix_a(proj, a_norm_g, a_ws, bias):
    n = proj.shape[0]
    t = 512
    return pl.pallas_call(
        functools.partial(_mix_a_kernel, nchunk=t // CHUNK),
        grid=(n // t,),
        in_specs=[
            pl.BlockSpec((t, GROUP_W), lambda i: (i, 0)),
            pl.BlockSpec((t, GROUP_W), lambda i: (i, 1)),
            pl.BlockSpec((1, GROUP_W), lambda i: (0, 0)),
            pl.BlockSpec((A_HEADS, CHUNK, CHUNK), lambda i: (0, 0, 0)),
            pl.BlockSpec((CHUNK, GROUP_W), lambda i: (0, 0)),
        ],
        out_specs=pl.BlockSpec((t, GROUP_W), lambda i: (i, 0)),
        out_shape=jax.ShapeDtypeStruct((n, GROUP_W), BF16),
        scratch_shapes=[pltpu.VMEM((t, GROUP_W), BF16)],
        compiler_params=_params(("arbitrary",), 32),
        name="mix_a",
    )(proj, proj, a_norm_g, a_ws, bias)


def _mix_b_kernel(ap_ref, ac_ref, an_ref, gp_ref, gc_ref, gn_ref, w_ref, cb_ref, lg_ref, lb_ref,
                  pw_ref, o_ref, pad_ref, rot_ref, conv_ref, *, t, n_prompt, seq_p, seq_s):
    row0 = pl.program_id(0) * t
    in_prompt = row0 < n_prompt
    off = jnp.where(in_prompt, row0, row0 - n_prompt)
    seq = jnp.where(in_prompt, seq_p, seq_s)
    has_prev = lax.rem(off, seq) != 0
    has_next = lax.rem(off + t, seq) != 0

    def glu(a_ref, g_ref):
        return a_ref[...] * jax.nn.sigmoid(g_ref[...])

    pad_ref[0:CONV_HALO, :] = jnp.where(has_prev, glu(ap_ref, gp_ref), 0.0)
    pad_ref[CONV_HALO:CONV_HALO + t, :] = glu(ac_ref, gc_ref)
    pad_ref[CONV_HALO + t:2 * CONV_HALO + t, :] = jnp.where(has_next, glu(an_ref, gn_ref), 0.0)

    span = rot_ref.shape[1]
    for p in range(1, SUBLANES):
        rot_ref[p - 1, :, :] = pad_ref[p:p + span, :]

    base = CONV_HALO - CONV_W // 2
    for r in range(t // CONV_SUB):
        acc = jnp.broadcast_to(cb_ref[...], (CONV_SUB, GROUP_W))
        for j in range(CONV_W):
            p = (base + j) % SUBLANES
            start = r * CONV_SUB + base + j - p
            tap = pad_ref[start:start + CONV_SUB, :] if p == 0 else rot_ref[p - 1, start:start + CONV_SUB, :]
            acc = acc + tap * w_ref[j:j + 1, :]
        conv_ref[r * CONV_SUB:(r + 1) * CONV_SUB, :] = acc

    c = conv_ref[...]
    mu = jnp.mean(c, axis=-1, keepdims=True)
    dlt = c - mu
    y = dlt * lax.rsqrt(jnp.mean(dlt * dlt, axis=-1, keepdims=True) + EPS) * lg_ref[...] + lb_ref[...]
    o_ref[...] = _dot(jax.nn.silu(y).astype(BF16), pw_ref[...]).astype(BF16)


def _mix_b(proj, conv_w, conv_b, ln_g, ln_b, pw, n_prompt, seq_p, seq_s):
    n = proj.shape[0]
    t = _tile(math.gcd(seq_p, seq_s), 256)
    hb = t // CONV_HALO
    nhb = n // CONV_HALO
    cur = lambda c: pl.BlockSpec((t, GROUP_W), lambda i: (i, c))
    prev = lambda c: pl.BlockSpec((CONV_HALO, GROUP_W), lambda i: (jnp.maximum(i * hb - 1, 0), c))
    nxt = lambda c: pl.BlockSpec((CONV_HALO, GROUP_W), lambda i: (jnp.minimum((i + 1) * hb, nhb - 1), c))
    vec = pl.BlockSpec((1, GROUP_W), lambda i: (0, 0))
    return pl.pallas_call(
        functools.partial(_mix_b_kernel, t=t, n_prompt=n_prompt, seq_p=seq_p, seq_s=seq_s),
        grid=(n // t,),
        in_specs=[prev(2), cur(2), nxt(2), prev(3), cur(3), nxt(3),
                  pl.BlockSpec((CONV_W, GROUP_W), lambda i: (0, 0)), vec, vec, vec,
                  pl.BlockSpec((GROUP_W, GROUP_W), lambda i: (0, 0))],
        out_specs=pl.BlockSpec((t, GROUP_W), lambda i: (i, 0)),
        out_shape=jax.ShapeDtypeStruct((n, GROUP_W), BF16),
        scratch_shapes=[pltpu.VMEM((t + 2 * CONV_HALO, GROUP_W), F32),
                        pltpu.VMEM((SUBLANES - 1, t + 2 * CONV_HALO - SUBLANES, GROUP_W), F32),
                        pltpu.VMEM((t, GROUP_W), F32)],
        compiler_params=_params(("arbitrary",), 32),
        name="mix_b",
    )(proj, proj, proj, proj, proj, proj, conv_w, conv_b, ln_g, ln_b, pw)


def _group_mean_sq(x, gm_ref):
    s = x * x
    s_hi = s.astype(BF16)
    s_lo = (s - s_hi.astype(F32)).astype(BF16)
    return _dot(s_hi, gm_ref[...]) + _dot(s_lo, gm_ref[...])


def _swap16(x):
    n = x.shape[-1]
    lane = lax.broadcasted_iota(jnp.int32, x.shape, x.ndim - 1)
    first = lax.rem(lane, 32) < 16
    return jnp.where(first, pltpu.roll(x, n - 16, x.ndim - 1), pltpu.roll(x, 16, x.ndim - 1))


def _prep_prompt_kernel(*refs):
    q_ref, k_ref, v_ref, gq_ref, gk_ref, gm_ref = refs[:6]
    qo_ref, ko_ref, vo_ref, nk_ref, nv_ref = refs[-5:]
    q = q_ref[...]
    k = k_ref[...]
    v = v_ref[...]
    qn = q * lax.rsqrt(_group_mean_sq(q, gm_ref) + EPS) * gq_ref[...]
    kn = k * lax.rsqrt(_group_mean_sq(k, gm_ref) + EPS) * gk_ref[...]
    nseq, seq = nk_ref.shape[0], nk_ref.shape[1]
    for b in range(nseq):
        rows = slice(b * seq, (b + 1) * seq)
        for h in range(C_HEADS):
            cols = slice(h * C_DV, (h + 1) * C_DV)
            nk_ref[b, :, h, :] = kn[rows, cols]
            nv_ref[b, :, h, :] = v[rows, cols]
    qo_ref[...] = (qn * Q_SCALE).astype(BF16)
    ko_ref[...] = kn.astype(BF16)
    vo_ref[...] = v.astype(BF16)


def _prep_latent_kernel(q_ref, k_ref, v_ref, gq_ref, gk_ref, gm_ref, cos_ref, sin_ref, qo_ref, ko_ref, vo_ref):
    q = q_ref[...]
    k = k_ref[...]
    qn = q * lax.rsqrt(_group_mean_sq(q, gm_ref) + EPS) * gq_ref[...]
    kn = k * lax.rsqrt(_group_mean_sq(k, gm_ref) + EPS) * gk_ref[...]
    cos = cos_ref[...]
    sin = sin_ref[...]
    qr = qn * cos + _swap16(qn) * sin
    kr = kn * cos + _swap16(kn) * sin
    qo_ref[...] = (qr * Q_SCALE).astype(BF16)
    ko_ref[...] = kr.astype(BF16)
    vo_ref[...] = v_ref[...].astype(BF16)


def _prep(proj, gq, gk, gm, row0, nrows, rope, past=0, cache_out=None):
    if rope is None:
        t = _tile(nrows, max(512, cache_out[2]))
    else:
        t = _tile(math.gcd(nrows, past), 512)
    rb = row0 // t
    col = lambda c: pl.BlockSpec((t, GROUP_W), lambda i: (rb + i, c))
    vec = pl.BlockSpec((1, GROUP_W), lambda i: (0, 0))
    out = pl.BlockSpec((t, GROUP_W), lambda i: (i, 0))
    in_specs = [col(4), col(5), col(6), vec, vec, pl.BlockSpec((GROUP_W, GROUP_W), lambda i: (0, 0))]
    args = [proj, proj, proj, gq, gk, gm]
    shp = lambda dt: jax.ShapeDtypeStruct((nrows, GROUP_W), dt)
    aliases = {}
    if rope is None:
        layer, depth, seq, prev = cache_out
        nseq = t // seq
        cache = pl.BlockSpec((nseq, None, seq, C_HEADS, C_DV), lambda i: (i, layer, 0, 0, 0))
        cache_shape = jax.ShapeDtypeStruct((nrows // seq, depth, seq, C_HEADS, C_DV), F32)
        body, out_specs, out_shape = _prep_prompt_kernel, [out] * 3 + [cache] * 2, [shp(BF16)] * 3 + [cache_shape] * 2
        if prev is not None:
            in_specs += [pl.BlockSpec(memory_space=pl.ANY)] * 2
            args += list(prev)
            aliases = {6: 3, 7: 4}
    else:
        cos, sin = rope
        seq = cos.shape[0]
        npos = seq // t
        tab = pl.BlockSpec((t, GROUP_W), lambda i: (lax.rem(i, npos), 0))
        in_specs += [tab, tab]
        args += [cos, sin]
        kv = pl.BlockSpec((None, t, GROUP_W), lambda i: (i // npos, past // t + lax.rem(i, npos), 0))
        kv_shape = jax.ShapeDtypeStruct((nrows // seq, past + seq, GROUP_W), BF16)
        body, out_specs, out_shape = _prep_latent_kernel, [out, kv, kv], [shp(BF16), kv_shape, kv_shape]
    return pl.pallas_call(
        body, grid=(nrows // t,), in_specs=in_specs, out_specs=out_specs, out_shape=out_shape,
        input_output_aliases=aliases,
        compiler_params=_params(("arbitrary",), 32),
        name="qkv_prep_prompt" if rope is None else "qkv_prep_latent",
    )(*args)


def _cache_fill_kernel(ck_ref, cv_ref, k_in, v_in, ko_ref, vo_ref):
    for h in range(C_HEADS):
        cols = slice(h * C_DV, (h + 1) * C_DV)
        ko_ref[:, cols] = ck_ref[:, h, :].astype(BF16)
        vo_ref[:, cols] = cv_ref[:, h, :].astype(BF16)


def _cache_fill(cache_k, cache_v, layer, k_all, v_all):
    nbs, _, past, nh, dv = cache_k.shape
    cache = pl.BlockSpec((None, None, past, nh, dv), lambda b: (b, layer, 0, 0, 0))
    out = pl.BlockSpec((None, past, GROUP_W), lambda b: (b, 0, 0))
    hbm = pl.BlockSpec(memory_space=pl.ANY)
    return pl.pallas_call(
        _cache_fill_kernel, grid=(nbs,), in_specs=[cache, cache, hbm, hbm], out_specs=[out, out],
        out_shape=[jax.ShapeDtypeStruct(k_all.shape, BF16)] * 2,
        input_output_aliases={2: 0, 3: 1},
        compiler_params=_params(("arbitrary",), 32),
        name="cache_fill",
    )(cache_k, cache_v, k_all, v_all)


def _attn_kernel(*refs, lam_init):
    lam_ref, q_ref, k_ref, v_ref, sg_ref = refs[:5]
    lp = lam_ref[...]
    lam = (jnp.exp(jnp.sum(lp[0:1, :] * lp[1:2, :], axis=-1, keepdims=True))
           - jnp.exp(jnp.sum(lp[2:3, :] * lp[3:4, :], axis=-1, keepdims=True)) + lam_init)
    s_ref = refs[-1]
    o_ref = refs[-2]
    tq, lk = q_ref.shape[0], k_ref.shape[0]
    kc = _tile(lk, ATTN_KEY_CHUNK)
    lane = lax.broadcasted_iota(jnp.int32, (tq, C_DV), 1)
    first = lane < C_DK
    dims = (((1,), (1,)), ((), ()))
    head_cols = lambda h: slice(h * C_DV, (h + 1) * C_DV)

    def lane_groups(x):
        return [x[:, g * C_DV:(g + 1) * C_DV] for g in range(x.shape[1] // C_DV)]

    row_max, row_sum = {}, {}
    for t in range(C_HEADS + 2):
        ha, hb, hc = t, t - 1, t - 2
        do_a, do_b, do_c = ha < C_HEADS, 0 <= hb < C_HEADS, 0 <= hc
        if do_a:
            qh = q_ref[:, head_cols(ha)]
            zero = jnp.zeros_like(qh)
            q_maps = jnp.concatenate([jnp.where(first, qh, zero), jnp.where(first, zero, qh)], axis=0)
            part_max = [jnp.full((tq, C_DV), -jnp.inf, F32) for _ in range(2)]
        if do_b:
            part_sum = [jnp.zeros((tq, C_DV), F32) for _ in range(2)]
        if do_c:
            l1, l2 = row_sum[hc]
            r = lam * l1 * (1.0 / l2)
            acc = jnp.zeros((tq, C_DV), F32)
        for c in range(lk // kc):
            keys = slice(c * kc, (c + 1) * kc)
            if do_a:
                s_maps = lax.dot_general(q_maps, k_ref[keys, head_cols(ha)], dims, preferred_element_type=F32)
                for m in range(2):
                    s = s_maps[m * tq:(m + 1) * tq, :]
                    s_ref[ha % 3, m, :, keys] = s
                    part_max[m] = jnp.maximum(part_max[m], functools.reduce(jnp.maximum, lane_groups(s)))
            if do_b:
                for m in range(2):
                    e = jnp.exp2(s_ref[hb % 3, m, :, keys] - row_max[hb][m])
                    s_ref[hb % 3, m, :, keys] = e
                    part_sum[m] = part_sum[m] + functools.reduce(jnp.add, lane_groups(e))
            if do_c:
                a = (s_ref[hc % 3, 0, :, keys] - r * s_ref[hc % 3, 1, :, keys]).astype(BF16)
                acc = acc + _dot(a, v_ref[keys, head_cols(hc)])
        if do_a:
            row_max[ha] = [jnp.max(p, axis=-1, keepdims=True) for p in part_max]
        if do_b:
            row_sum[hb] = [jnp.sum(p, axis=-1, keepdims=True) for p in part_sum]
        if do_c:
            o = acc * (1.0 / l1)
            o = o * lax.rsqrt(jnp.mean(o * o, axis=-1, keepdims=True) + EPS) * sg_ref[...]
            o_ref[:, head_cols(hc)] = (o * (1.0 - lam_init)).astype(BF16)


def _attention(c_lambda, q, k, v, subln_g, lam_init, n_total, row0, prev_out):
    nb, lk, _ = k.shape
    lq = q.shape[0] // nb
    tq = _tile(lq, 256)
    nq = lq // tq
    rb = row0 // tq
    in_specs = [
        pl.BlockSpec((4, C_DK), lambda b, t: (0, 0)),
        pl.BlockSpec((tq, GROUP_W), lambda b, t: (b * nq + t, 0)),
        pl.BlockSpec((None, lk, GROUP_W), lambda b, t: (b, 0, 0)),
        pl.BlockSpec((None, lk, GROUP_W), lambda b, t: (b, 0, 0)),
        pl.BlockSpec((1, C_DV), lambda b, t: (0, 0)),
    ]
    args = [c_lambda, q, k, v, subln_g]
    aliases = {}
    if prev_out is not None:
        in_specs.append(pl.BlockSpec(memory_space=pl.ANY))
        args.append(prev_out)
        aliases = {5: 0}
    return pl.pallas_call(
        functools.partial(_attn_kernel, lam_init=lam_init),
        grid=(nb, nq),
        in_specs=in_specs,
        out_specs=pl.BlockSpec((tq, GROUP_W), lambda b, t: (rb + b * nq + t, 0)),
        out_shape=jax.ShapeDtypeStruct((n_total, GROUP_W), BF16),
        scratch_shapes=[pltpu.VMEM((3, 2, tq, lk), F32)],
        input_output_aliases=aliases,
        compiler_params=_params(("arbitrary", "arbitrary"), 56),
        name="diff_attn",
    )(*args)


def _dft_kernel(*refs, scale):
    dx_ref, wc_ref, c_ref, s_ref, dl_ref = refs[:5]
    o_ref, pq_ref = refs[-2], refs[-1]

    @pl.when(pl.program_id(1) == 0)
    def _():
        pq_ref[...] = _dot(dx_ref[...].astype(BF16), wc_ref[...]).astype(BF16)

    y = _dot(c_ref[...], pq_ref[:, 0:GROUP_W]) - _dot(s_ref[...], pq_ref[:, GROUP_W:2 * GROUP_W])
    o_ref[...] = _dot((y * scale).astype(BF16), dl_ref[...]).astype(BF16)


def _fourier(proj, wc, cl, sl, d_lin, nb, seq, n_total, row0, prev_out):
    tr = _tile(seq, 512)
    nr = seq // tr
    sb = row0 // seq
    rb = row0 // tr
    in_specs = [
        pl.BlockSpec((seq, GROUP_W), lambda b, r: (sb + b, N_GROUPS - 1)),
        pl.BlockSpec((GROUP_W, 2 * GROUP_W), lambda b, r: (0, 0)),
        pl.BlockSpec((tr, seq), lambda b, r: (r, 0)),
        pl.BlockSpec((tr, seq), lambda b, r: (r, 0)),
        pl.BlockSpec((GROUP_W, GROUP_W), lambda b, r: (0, 0)),
    ]
    args = [proj, wc, cl, sl, d_lin]
    aliases = {}
    if prev_out is not None:
        in_specs.append(pl.BlockSpec(memory_space=pl.ANY))
        args.append(prev_out)
        aliases = {5: 0}
    return pl.pallas_call(
        functools.partial(_dft_kernel, scale=1.0 / math.sqrt(seq * D_GW)),
        grid=(nb, nr),
        in_specs=in_specs,
        out_specs=pl.BlockSpec((tr, GROUP_W), lambda b, r: (rb + b * nr + r, 0)),
        out_shape=jax.ShapeDtypeStruct((n_total, GROUP_W), BF16),
        scratch_shapes=[pltpu.VMEM((seq, 2 * GROUP_W), BF16)],
        input_output_aliases=aliases,
        compiler_params=_params(("arbitrary", "arbitrary"), 56),
        name="fourier_mix",
    )(*args)


def _outproj_kernel(x_ref, mod_ref, a_ref, b_ref, c_ref, d_ref, w_ref, o_ref):
    y = _dot(a_ref[...], w_ref[0:GROUP_W, :])
    y += _dot(b_ref[...], w_ref[GROUP_W:2 * GROUP_W, :])
    y += _dot(c_ref[...], w_ref[2 * GROUP_W:3 * GROUP_W, :])
    y += _dot(d_ref[...], w_ref[3 * GROUP_W:4 * GROUP_W, :])
    o_ref[...] = x_ref[...] + mod_ref[5:6, :] * y


def _outproj(x, modl, oa, ob, oc, od, w_out, layer, cond_of):
    n, d = x.shape
    tm = cond_of.tile(512)
    mix = pl.BlockSpec((tm, GROUP_W), lambda i: (i, 0))
    return pl.pallas_call(
        _outproj_kernel,
        grid=(n // tm,),
        in_specs=[
            pl.BlockSpec((tm, d), lambda i: (i, 0)),
            pl.BlockSpec((None, N_MOD, d), lambda i: (cond_of(i, tm), 0, 0)),
            mix, mix, mix, mix,
            pl.BlockSpec((None, 4 * GROUP_W, d), lambda i: (layer, 0, 0)),
        ],
        out_specs=pl.BlockSpec((tm, d), lambda i: (i, 0)),
        out_shape=jax.ShapeDtypeStruct((n, d), F32),
        compiler_params=_params(("arbitrary",), 48),
        name="out_proj",
    )(x, modl, oa, ob, oc, od, w_out)


def _dft_tables(n, split=128):
    def trig(cols, stride):
        j = lax.broadcasted_iota(jnp.int32, (n, cols), 0)
        k = lax.broadcasted_iota(jnp.int32, (n, cols), 1) * stride
        ang = (2.0 * math.pi / n) * lax.rem(j * k, n).astype(F32)
        return jnp.cos(ang), jnp.sin(ang)

    if n % split or n <= split:
        return trig(n, 1)
    (ch, sh), (cl, sl) = trig(n // split, split), trig(split, 1)
    ch, sh, cl, sl = ch[:, :, None], sh[:, :, None], cl[:, None, :], sl[:, None, :]
    return (ch * cl - sh * sl).reshape(n, n), (sh * cl + ch * sl).reshape(n, n)


def _channel_dft_table():
    c, s = _dft_tables(D_GW)
    eye = jnp.eye(GROUP_W // D_GW, dtype=F32)
    return jnp.concatenate([jnp.kron(eye, c), jnp.kron(eye, s)], axis=1).astype(BF16)


def _rope_tables(n_tokens):
    rows = n_tokens // GRID_W
    row = jnp.repeat(jnp.arange(rows, dtype=F32), GRID_W)
    col = jnp.tile(jnp.arange(GRID_W, dtype=F32), rows)
    freqs = ROPE_THETA ** (-jnp.arange(ROPE_NF, dtype=F32) / ROPE_NF)
    ang_r = row[:, None] * freqs
    ang_c = col[:, None] * freqs

    def lanes(fr, fc, sign):
        grp = jnp.concatenate([sign * fr, fr, sign * fc, fc], axis=-1)
        return jnp.tile(grp, (1, GROUP_W // C_DK))

    cos = lanes(jnp.cos(ang_r), jnp.cos(ang_c), 1.0)
    sin = lanes(jnp.sin(ang_r), jnp.sin(ang_c), -1.0)
    return cos, sin


class _CondOf:
    def __init__(self, n_prompt, seq_s):
        self.n_prompt = n_prompt
        self.seq_s = seq_s

    def tile(self, target):
        return _tile(math.gcd(self.n_prompt, self.seq_s), target)

    def __call__(self, i, tm):
        r = i * tm
        return jnp.where(r < self.n_prompt, 0, 1 + (r - self.n_prompt) // self.seq_s)


def kernel(x_prompt, x_sample, c, cache_k, cache_v, c_ctx, w_mod, b_mod, norm_g, w_ff1_in, w_ff1_down, w_ff2_in, w_ff2_down, w_in, w_out, a_norm_g, a_ws, a_bs, b_conv_w, b_conv_b, b_ln_g, b_ln_b, b_pw, c_qnorm_g, c_knorm_g, c_lambda, c_subln_g, d_lin):
    nbp, seq_p, d = x_prompt.shape
    nbs, seq_s, _ = x_sample.shape
    depth = w_mod.shape[0]
    past = cache_k.shape[2]
    n_p, n_s = nbp * seq_p, nbs * seq_s
    n = n_p + n_s
    assert n_p % seq_s == 0 and seq_p % CHUNK == 0 and seq_s % CHUNK == 0
    assert 1 + nbs <= COND_PAD
    cond_of = _CondOf(n_p, seq_s)

    cond = jnp.zeros((COND_PAD, d), F32).at[0].set(c_ctx).at[1:1 + nbs].set(c)
    mods = _modulation(cond, w_mod, b_mod).reshape(depth, COND_PAD, N_MOD, d)

    wc = _channel_dft_table()
    tabs_p = [t.astype(BF16) for t in _dft_tables(seq_p)]
    tabs_s = [t.astype(BF16) for t in _dft_tables(seq_s)]
    rope = _rope_tables(seq_s)
    gidx = jnp.arange(GROUP_W) // C_DK
    gm = jnp.where(gidx[:, None] == gidx[None, :], 1.0 / C_DK, 0.0).astype(BF16)

    assert w_ff1_down.shape[1] % FFN_TF == 0
    wu1, wd1 = w_ff1_in, w_ff1_down
    wu2, wd2 = w_ff2_in, w_ff2_down
    w_in_b, w_out_b = w_in.astype(BF16), w_out.astype(BF16)

    new_kv = None
    for l in range(depth):
        lam_init = 0.8 - 0.6 * math.exp(-0.3 * l)
        modl = mods[l]
        ng = norm_g[l]
        ff1 = (modl, ng[0:1], wu1, wd1, l, 0, cond_of)
        if l == 0:
            x = _ffn(x_prompt.reshape(n_p, d), *ff1, out_total=n)
            x = _ffn(x_sample.reshape(n_s, d), *ff1, tok_row0=n_p, out_total=n, out_row0=n_p, prev_out=x)
        else:
            x = _ffn(x, *ff1)

        proj = _inproj(x, modl, ng[1:2], w_in_b, l, cond_of)

        bias_a = jnp.repeat(a_bs[l].T, CHUNK, axis=1)
        out_a = _mix_a(proj, a_norm_g[l][None], a_ws[l].astype(BF16), bias_a)

        out_b = _mix_b(proj, b_conv_w[l], b_conv_b[l][None], b_ln_g[l][None], b_ln_b[l][None],
                       b_pw[l].astype(BF16), n_p, seq_p, seq_s)

        gq = jnp.tile(c_qnorm_g[l], GROUP_W // C_DK)[None]
        gk = jnp.tile(c_knorm_g[l], GROUP_W // C_DK)[None]
        sg = c_subln_g[l][None]
        qp, kp, vp, *new_kv = _prep(proj, gq, gk, gm, 0, n_p, None, cache_out=(l, depth, seq_p, new_kv))
        qs, k_all, v_all = _prep(proj, gq, gk, gm, n_p, n_s, rope, past)
        k_all, v_all = _cache_fill(cache_k, cache_v, l, k_all, v_all)
        out_c = _attention(c_lambda[l], qp, kp.reshape(nbp, seq_p, GROUP_W), vp.reshape(nbp, seq_p, GROUP_W),
                           sg, lam_init, n, 0, None)
        out_c = _attention(c_lambda[l], qs, k_all, v_all, sg, lam_init, n, n_p, out_c)

        dl = d_lin[l].astype(BF16)
        out_d = _fourier(proj, wc, tabs_p[0], tabs_p[1], dl, nbp, seq_p, n, 0, None)
        out_d = _fourier(proj, wc, tabs_s[0], tabs_s[1], dl, nbs, seq_s, n, n_p, out_d)

        x = _outproj(x, modl, out_a, out_b, out_c, out_d, w_out_b, l, cond_of)
        ff2 = (modl, ng[2:3], wu2, wd2, l, 6, cond_of)
        if l == depth - 1:
            y_p = _ffn(x, *ff2, nrows=n_p)
            y_s = _ffn(x, *ff2, in_row0=n_p, nrows=n_s, tok_row0=n_p)
        else:
            x = _ffn(x, *ff2)

    new_k, new_v = new_kv
    return (y_p.reshape(nbp, seq_p, d), y_s.reshape(nbs, seq_s, d), new_k, new_v)
```

```python
import functools
import math

import jax
import jax.numpy as jnp
from jax import lax
from jax.experimental import pallas as pl
from jax.experimental.pallas import tpu as pltpu

F32 = jnp.float32
BF16 = jnp.bfloat16

EPS = 1e-6
GROUP_W = 512
N_GROUPS = 8
CHUNK = 128
A_HEADS = 4
CONV_W = 31
CONV_HALO = 16
CONV_SUB = 32
SUBLANES = 8
C_HEADS = 4
C_DK = 64
C_DV = 128
Q_SCALE = C_DK ** -0.5 * math.log2(math.e)
ATTN_KEY_CHUNK = 512
ROPE_NF = 16
ROPE_THETA = 10000.0
GRID_W = 64
D_GW = 128
N_MOD = 9
COND_PAD = 8
MIB = 1024 * 1024
FFN_TM = 1024
FFN_TF = 512
FFN_NSPLIT = 4
FFN_VMEM_MIB = 60
NORM_ROWS = 16
NORM_UNROLL = 8


def _tile(n, target):
    t = target
    while n % t:
        t //= 2
    return t


def _params(sem, vmem_mib):
    return pltpu.CompilerParams(dimension_semantics=sem, vmem_limit_bytes=vmem_mib * MIB)


def _dot(a, b):
    return jnp.dot(a, b, preferred_element_type=F32)


def _mod_kernel(c_ref, w_ref, b_ref, o_ref):
    s = jax.nn.silu(c_ref[...]).astype(BF16)
    o_ref[...] = _dot(s, w_ref[...].astype(BF16)) + b_ref[...]


def _modulation(cond, w_mod, b_mod):
    depth, d, nd = w_mod.shape
    tn = _tile(nd, 1024)
    return pl.pallas_call(
        _mod_kernel,
        grid=(depth, nd // tn),
        in_specs=[
            pl.BlockSpec((COND_PAD, d), lambda l, j: (0, 0)),
            pl.BlockSpec((None, d, tn), lambda l, j: (l, 0, j)),
            pl.BlockSpec((None, 1, tn), lambda l, j: (l, 0, j)),
        ],
        out_specs=pl.BlockSpec((None, COND_PAD, tn), lambda l, j: (l, 0, j)),
        out_shape=jax.ShapeDtypeStruct((depth, COND_PAD, nd), F32),
        compiler_params=_params(("arbitrary", "arbitrary"), 40),
        name="modulation",
    )(cond, w_mod, b_mod.reshape(depth, 1, nd))


def _ada_norm_rows(x_ref, h_ref, g, shift, scale, copy_ref=None):
    gain = g * (1.0 + scale)

    def body(c, carry):
        rows = pl.ds(pl.multiple_of(c * NORM_ROWS, NORM_ROWS), NORM_ROWS)
        x = x_ref[rows, :]
        y = x * lax.rsqrt(jnp.mean(x * x, axis=-1, keepdims=True) + EPS)
        h_ref[rows, :] = (y * gain + shift).astype(BF16)
        if copy_ref is not None:
            copy_ref[rows, :] = x
        return carry

    lax.fori_loop(0, x_ref.shape[0] // NORM_ROWS, body, 0, unroll=NORM_UNROLL)


def _ffn_kernel(*refs, s, n_split, first_tile):
    x_hbm, mod_ref, g_ref, wa_ref, wg_ref, wd_ref = refs[:6]
    o_ref, h_ref, x_ref, x_sem = refs[-4:]
    i, f = pl.program_id(0), pl.program_id(1)
    tm = x_ref.shape[0]

    def x_copy(tile):
        rows = pl.ds(pl.multiple_of((first_tile + tile) * tm, tm), tm)
        return pltpu.make_async_copy(x_hbm.at[rows, :], x_ref, x_sem)

    @pl.when((i == 0) & (f == 0))
    def _():
        x_copy(0).start()

    @pl.when(f == 0)
    def _():
        x_copy(i).wait()
        _ada_norm_rows(x_ref, h_ref, g_ref[...], mod_ref[s:s + 1, :], mod_ref[s + 1:s + 2, :], copy_ref=o_ref)

    @pl.when((f == 1) & (i + 1 < pl.num_programs(0)))
    def _():
        x_copy(i + 1).start()

    h = h_ref[...]
    a = _dot(h, wa_ref[...].astype(BF16))
    g = _dot(h, wg_ref[...].astype(BF16))
    act = (jax.nn.silu(g) * a).astype(BF16)
    gate = 0.5 * mod_ref[s + 2:s + 3, :]
    cw = o_ref.shape[1] // n_split
    for c in range(n_split):
        cols = slice(c * cw, (c + 1) * cw)
        o_ref[:, cols] += gate[:, cols] * _dot(act, wd_ref[:, cols].astype(BF16))


def _ffn(x, modl, g, w_up, w_down, layer, s, cond_of, *, in_row0=0, nrows=None, tok_row0=0,
         out_total=None, out_row0=0, prev_out=None):
    d = x.shape[1]
    nrows = x.shape[0] if nrows is None else nrows
    out_total = nrows if out_total is None else out_total
    ff = w_down.shape[1]
    tm = cond_of.tile(FFN_TM)
    tf = FFN_TF
    nf = ff // tf
    assert nf >= 2
    ib, tb, ob = in_row0 // tm, tok_row0 // tm, out_row0 // tm
    in_specs = [
        pl.BlockSpec(memory_space=pl.ANY),
        pl.BlockSpec((None, N_MOD, d), lambda i, f: (cond_of(tb + i, tm), 0, 0)),
        pl.BlockSpec((1, d), lambda i, f: (0, 0)),
        pl.BlockSpec((None, d, tf), lambda i, f: (layer, 0, f)),
        pl.BlockSpec((None, d, tf), lambda i, f: (layer, 0, f + nf)),
        pl.BlockSpec((None, tf, d), lambda i, f: (layer, f, 0)),
    ]
    args = [x, modl, g, w_up, w_up, w_down]
    aliases = {}
    if prev_out is not None:
        in_specs.append(pl.BlockSpec(memory_space=pl.ANY))
        args.append(prev_out)
        aliases = {6: 0}
    return pl.pallas_call(
        functools.partial(_ffn_kernel, s=s, n_split=FFN_NSPLIT, first_tile=ib),
        grid=(nrows // tm, nf),
        in_specs=in_specs,
        out_specs=pl.BlockSpec((tm, d), lambda i, f: (ob + i, 0)),
        out_shape=jax.ShapeDtypeStruct((out_total, d), F32),
        scratch_shapes=[pltpu.VMEM((tm, d), BF16), pltpu.VMEM((tm, d), F32), pltpu.SemaphoreType.DMA(())],
        input_output_aliases=aliases,
        compiler_params=_params(("arbitrary", "arbitrary"), FFN_VMEM_MIB),
        name="ffn",
    )(*args)


def _inproj_kernel(x0_ref, xn_ref, mod0_ref, modn_ref, g_ref, w_ref, o_ref, h_even_ref, h_odd_ref):
    i, j = pl.program_id(0), pl.program_id(1)
    g = g_ref[...]

    @pl.when((i == 0) & (j == 0))
    def _():
        _ada_norm_rows(x0_ref, h_even_ref, g, mod0_ref[3:4, :], mod0_ref[4:5, :])

    rows_per_step = xn_ref.shape[0]

    def step(h_cur_ref, h_next_ref):
        gain = g * (1.0 + modn_ref[4:5, :])
        shift = modn_ref[3:4, :]
        for c in range(rows_per_step // NORM_ROWS):
            x = xn_ref[c * NORM_ROWS:(c + 1) * NORM_ROWS, :]
            y = x * lax.rsqrt(jnp.mean(x * x, axis=-1, keepdims=True) + EPS)
            dst = pl.ds(pl.multiple_of(j * rows_per_step + c * NORM_ROWS, NORM_ROWS), NORM_ROWS)
            h_next_ref[dst, :] = (y * gain + shift).astype(BF16)
        o_ref[...] = _dot(h_cur_ref[...], w_ref[...].astype(BF16))

    @pl.when(lax.rem(i, 2) == 0)
    def _():
        step(h_even_ref, h_odd_ref)

    @pl.when(lax.rem(i, 2) == 1)
    def _():
        step(h_odd_ref, h_even_ref)


def _inproj(x, modl, g, w_in, layer, cond_of):
    n, d = x.shape
    nc = w_in.shape[2]
    tm = cond_of.tile(1024)
    tn = _tile(nc, 1024)
    nj = nc // tn
    rows = tm // nj
    last = n // tm - 1
    nxt = lambda i: jnp.minimum(i + 1, last)
    return pl.pallas_call(
        _inproj_kernel,
        grid=(n // tm, nj),
        in_specs=[
            pl.BlockSpec((tm, d), lambda i, j: (0, 0)),
            pl.BlockSpec((rows, d), lambda i, j: (nxt(i) * nj + j, 0)),
            pl.BlockSpec((None, N_MOD, d), lambda i, j: (cond_of(0, tm), 0, 0)),
            pl.BlockSpec((None, N_MOD, d), lambda i, j: (cond_of(nxt(i), tm), 0, 0)),
            pl.BlockSpec((1, d), lambda i, j: (0, 0)),
            pl.BlockSpec((None, d, tn), lambda i, j: (layer, 0, j)),
        ],
        out_specs=pl.BlockSpec((tm, tn), lambda i, j: (i, j)),
        out_shape=jax.ShapeDtypeStruct((n, nc), F32),
        scratch_shapes=[pltpu.VMEM((tm, d), BF16), pltpu.VMEM((tm, d), BF16)],
        compiler_params=_params(("arbitrary", "arbitrary"), 56),
        name="in_proj",
    )(x, x, modl, modl, g, w_in)


def _mix_a_kernel(u_ref, v_ref, g_ref, ws_ref, bias_ref, o_ref, vv_ref, *, nchunk):
    v = jax.nn.gelu(v_ref[...])
    vv = v * lax.rsqrt(jnp.mean(v * v, axis=-1, keepdims=True) + EPS) * g_ref[...]
    vv_ref[...] = vv.astype(BF16)
    for n in range(nchunk):
        rows = slice(n * CHUNK, (n + 1) * CHUNK)
        for h in range(A_HEADS):
            cols = slice(h * CHUNK, (h + 1) * CHUNK)
            sp = _dot(ws_ref[h], vv_ref[rows, cols]) + bias_ref[:, cols]
            o_ref[rows, cols] = (jax.nn.gelu(u_ref[rows, cols]) * sp).astype(BF16)


def _mix_a(proj, a_norm_g, a_ws, bias):
    n = proj.shape[0]
    t = 512
    return pl.pallas_call(
        functools.partial(_mix_a_kernel, nchunk=t // CHUNK),
        grid=(n // t,),
        in_specs=[
            pl.BlockSpec((t, GROUP_W), lambda i: (i, 0)),
            pl.BlockSpec((t, GROUP_W), lambda i: (i, 1)),
            pl.BlockSpec((1, GROUP_W), lambda i: (0, 0)),
            pl.BlockSpec((A_HEADS, CHUNK, CHUNK), lambda i: (0, 0, 0)),
            pl.BlockSpec((CHUNK, GROUP_W), lambda i: (0, 0)),
        ],
        out_specs=pl.BlockSpec((t, GROUP_W), lambda i: (i, 0)),
        out_shape=jax.ShapeDtypeStruct((n, GROUP_W), BF16),
        scratch_shapes=[pltpu.VMEM((t, GROUP_W), BF16)],
        compiler_params=_params(("arbitrary",), 32),
        name="mix_a",
    )(proj, proj, a_norm_g, a_ws, bias)


def _mix_b_kernel(ap_ref, ac_ref, an_ref, gp_ref, gc_ref, gn_ref, w_ref, cb_ref, lg_ref, lb_ref,
                  pw_ref, o_ref, pad_ref, rot_ref, conv_ref, *, t, n_prompt, seq_p, seq_s):
    row0 = pl.program_id(0) * t
    in_prompt = row0 < n_prompt
    off = jnp.where(in_prompt, row0, row0 - n_prompt)
    seq = jnp.where(in_prompt, seq_p, seq_s)
    has_prev = lax.rem(off, seq) != 0
    has_next = lax.rem(off + t, seq) != 0

    def glu(a_ref, g_ref):
        return a_ref[...] * jax.nn.sigmoid(g_ref[...])

    pad_ref[0:CONV_HALO, :] = jnp.where(has_prev, glu(ap_ref, gp_ref), 0.0)
    pad_ref[CONV_HALO:CONV_HALO + t, :] = glu(ac_ref, gc_ref)
    pad_ref[CONV_HALO + t:2 * CONV_HALO + t, :] = jnp.where(has_next, glu(an_ref, gn_ref), 0.0)

    span = rot_ref.shape[1]
    for p in range(1, SUBLANES):
        rot_ref[p - 1, :, :] = pad_ref[p:p + span, :]

    base = CONV_HALO - CONV_W // 2
    for r in range(t // CONV_SUB):
        acc = jnp.broadcast_to(cb_ref[...], (CONV_SUB, GROUP_W))
        for j in range(CONV_W):
            p = (base + j) % SUBLANES
            start = r * CONV_SUB + base + j - p
            tap = pad_ref[start:start + CONV_SUB, :] if p == 0 else rot_ref[p - 1, start:start + CONV_SUB, :]
            acc = acc + tap * w_ref[j:j + 1, :]
        conv_ref[r * CONV_SUB:(r + 1) * CONV_SUB, :] = acc

    c = conv_ref[...]
    mu = jnp.mean(c, axis=-1, keepdims=True)
    dlt = c - mu
    y = dlt * lax.rsqrt(jnp.mean(dlt * dlt, axis=-1, keepdims=True) + EPS) * lg_ref[...] + lb_ref[...]
    o_ref[...] = _dot(jax.nn.silu(y).astype(BF16), pw_ref[...]).astype(BF16)


def _mix_b(proj, conv_w, conv_b, ln_g, ln_b, pw, n_prompt, seq_p, seq_s):
    n = proj.shape[0]
    t = _tile(math.gcd(seq_p, seq_s), 256)
    hb = t // CONV_HALO
    nhb = n // CONV_HALO
    cur = lambda c: pl.BlockSpec((t, GROUP_W), lambda i: (i, c))
    prev = lambda c: pl.BlockSpec((CONV_HALO, GROUP_W), lambda i: (jnp.maximum(i * hb - 1, 0), c))
    nxt = lambda c: pl.BlockSpec((CONV_HALO, GROUP_W), lambda i: (jnp.minimum((i + 1) * hb, nhb - 1), c))
    vec = pl.BlockSpec((1, GROUP_W), lambda i: (0, 0))
    return pl.pallas_call(
        functools.partial(_mix_b_kernel, t=t, n_prompt=n_prompt, seq_p=seq_p, seq_s=seq_s),
        grid=(n // t,),
        in_specs=[prev(2), cur(2), nxt(2), prev(3), cur(3), nxt(3),
                  pl.BlockSpec((CONV_W, GROUP_W), lambda i: (0, 0)), vec, vec, vec,
                  pl.BlockSpec((GROUP_W, GROUP_W), lambda i: (0, 0))],
        out_specs=pl.BlockSpec((t, GROUP_W), lambda i: (i, 0)),
        out_shape=jax.ShapeDtypeStruct((n, GROUP_W), BF16),
        scratch_shapes=[pltpu.VMEM((t + 2 * CONV_HALO, GROUP_W), F32),
                        pltpu.VMEM((SUBLANES - 1, t + 2 * CONV_HALO - SUBLANES, GROUP_W), F32),
                        pltpu.VMEM((t, GROUP_W), F32)],
        compiler_params=_params(("arbitrary",), 32),
        name="mix_b",
    )(proj, proj, proj, proj, proj, proj, conv_w, conv_b, ln_g, ln_b, pw)


def _group_mean_sq(x, gm_ref):
    s = x * x
    s_hi = s.astype(BF16)
    s_lo = (s - s_hi.astype(F32)).astype(BF16)
    return _dot(s_hi, gm_ref[...]) + _dot(s_lo, gm_ref[...])


def _swap16(x):
    n = x.shape[-1]
    lane = lax.broadcasted_iota(jnp.int32, x.shape, x.ndim - 1)
    first = lax.rem(lane, 32) < 16
    return jnp.where(first, pltpu.roll(x, n - 16, x.ndim - 1), pltpu.roll(x, 16, x.ndim - 1))


def _prep_prompt_kernel(*refs):
    q_ref, k_ref, v_ref, gq_ref, gk_ref, gm_ref = refs[:6]
    qo_ref, ko_ref, vo_ref, nk_ref, nv_ref = refs[-5:]
    q = q_ref[...]
    k = k_ref[...]
    v = v_ref[...]
    qn = q * lax.rsqrt(_group_mean_sq(q, gm_ref) + EPS) * gq_ref[...]
    kn = k * lax.rsqrt(_group_mean_sq(k, gm_ref) + EPS) * gk_ref[...]
    nseq, seq = nk_ref.shape[0], nk_ref.shape[1]
    for b in range(nseq):
        rows = slice(b * seq, (b + 1) * seq)
        for h in range(C_HEADS):
            cols = slice(h * C_DV, (h + 1) * C_DV)
            nk_ref[b, :, h, :] = kn[rows, cols]
            nv_ref[b, :, h, :] = v[rows, cols]
    qo_ref[...] = (qn * Q_SCALE).astype(BF16)
    ko_ref[...] = kn.astype(BF16)
    vo_ref[...] = v.astype(BF16)


def _prep_latent_kernel(q_ref, k_ref, v_ref, gq_ref, gk_ref, gm_ref, cos_ref, sin_ref, qo_ref, ko_ref, vo_ref):
    q = q_ref[...]
    k = k_ref[...]
    qn = q * lax.rsqrt(_group_mean_sq(q, gm_ref) + EPS) * gq_ref[...]
    kn = k * lax.rsqrt(_group_mean_sq(k, gm_ref) + EPS) * gk_ref[...]
    cos = cos_ref[...]
    sin = sin_ref[...]
    qr = qn * cos + _swap16(qn) * sin
    kr = kn * cos + _swap16(kn) * sin
    qo_ref[...] = (qr * Q_SCALE).astype(BF16)
    ko_ref[...] = kr.astype(BF16)
    vo_ref[...] = v_ref[...].astype(BF16)


def _prep(proj, gq, gk, gm, row0, nrows, rope, past=0, cache_out=None):
    if rope is None:
        t = _tile(nrows, max(512, cache_out[2]))
    else:
        t = _tile(math.gcd(nrows, past), 512)
    rb = row0 // t
    col = lambda c: pl.BlockSpec((t, GROUP_W), lambda i: (rb + i, c))
    vec = pl.BlockSpec((1, GROUP_W), lambda i: (0, 0))
    out = pl.BlockSpec((t, GROUP_W), lambda i: (i, 0))
    in_specs = [col(4), col(5), col(6), vec, vec, pl.BlockSpec((GROUP_W, GROUP_W), lambda i: (0, 0))]
    args = [proj, proj, proj, gq, gk, gm]
    shp = lambda dt: jax.ShapeDtypeStruct((nrows, GROUP_W), dt)
    aliases = {}
    if rope is None:
        layer, depth, seq, prev = cache_out
        nseq = t // seq
        cache = pl.BlockSpec((nseq, None, seq, C_HEADS, C_DV), lambda i: (i, layer, 0, 0, 0))
        cache_shape = jax.ShapeDtypeStruct((nrows // seq, depth, seq, C_HEADS, C_DV), F32)
        body, out_specs, out_shape = _prep_prompt_kernel, [out] * 3 + [cache] * 2, [shp(BF16)] * 3 + [cache_shape] * 2
        if prev is not None:
            in_specs += [pl.BlockSpec(memory_space=pl.ANY)] * 2
            args += list(prev)
            aliases = {6: 3, 7: 4}
    else:
        cos, sin = rope
        seq = cos.shape[0]
        npos = seq // t
        tab = pl.BlockSpec((t, GROUP_W), lambda i: (lax.rem(i, npos), 0))
        in_specs += [tab, tab]
        args += [cos, sin]
        kv = pl.BlockSpec((None, t, GROUP_W), lambda i: (i // npos, past // t + lax.rem(i, npos), 0))
        kv_shape = jax.ShapeDtypeStruct((nrows // seq, past + seq, GROUP_W), BF16)
        body, out_specs, out_shape = _prep_latent_kernel, [out, kv, kv], [shp(BF16), kv_shape, kv_shape]
    return pl.pallas_call(
        body, grid=(nrows // t,), in_specs=in_specs, out_specs=out_specs, out_shape=out_shape,
        input_output_aliases=aliases,
        compiler_params=_params(("arbitrary",), 32),
        name="qkv_prep_prompt" if rope is None else "qkv_prep_latent",
    )(*args)


def _cache_fill_kernel(ck_ref, cv_ref, k_in, v_in, ko_ref, vo_ref):
    for h in range(C_HEADS):
        cols = slice(h * C_DV, (h + 1) * C_DV)
        ko_ref[:, cols] = ck_ref[:, h, :].astype(BF16)
        vo_ref[:, cols] = cv_ref[:, h, :].astype(BF16)


def _cache_fill(cache_k, cache_v, layer, k_all, v_all):
    nbs, _, past, nh, dv = cache_k.shape
    cache = pl.BlockSpec((None, None, past, nh, dv), lambda b: (b, layer, 0, 0, 0))
    out = pl.BlockSpec((None, past, GROUP_W), lambda b: (b, 0, 0))
    hbm = pl.BlockSpec(memory_space=pl.ANY)
    return pl.pallas_call(
        _cache_fill_kernel, grid=(nbs,), in_specs=[cache, cache, hbm, hbm], out_specs=[out, out],
        out_shape=[jax.ShapeDtypeStruct(k_all.shape, BF16)] * 2,
        input_output_aliases={2: 0, 3: 1},
        compiler_params=_params(("arbitrary",), 32),
        name="cache_fill",
    )(cache_k, cache_v, k_all, v_all)


def _attn_kernel(*refs, lam_init):
    lam_ref, q_ref, k_ref, v_ref, sg_ref = refs[:5]
    lp = lam_ref[...]
    lam = (jnp.exp(jnp.sum(lp[0:1, :] * lp[1:2, :], axis=-1, keepdims=True))
           - jnp.exp(jnp.sum(lp[2:3, :] * lp[3:4, :], axis=-1, keepdims=True)) + lam_init)
    s_ref = refs[-1]
    o_ref = refs[-2]
    tq, lk = q_ref.shape[0], k_ref.shape[0]
    kc = _tile(lk, ATTN_KEY_CHUNK)
    lane = lax.broadcasted_iota(jnp.int32, (tq, C_DV), 1)
    first = lane < C_DK
    dims = (((1,), (1,)), ((), ()))
    head_cols = lambda h: slice(h * C_DV, (h + 1) * C_DV)

    def lane_groups(x):
        return [x[:, g * C_DV:(g + 1) * C_DV] for g in range(x.shape[1] // C_DV)]

    row_max, row_sum = {}, {}
    for t in range(C_HEADS + 2):
        ha, hb, hc = t, t - 1, t - 2
        do_a, do_b, do_c = ha < C_HEADS, 0 <= hb < C_HEADS, 0 <= hc
        if do_a:
            qh = q_ref[:, head_cols(ha)]
            zero = jnp.zeros_like(qh)
            q_maps = jnp.concatenate([jnp.where(first, qh, zero), jnp.where(first, zero, qh)], axis=0)
            part_max = [jnp.full((tq, C_DV), -jnp.inf, F32) for _ in range(2)]
        if do_b:
            part_sum = [jnp.zeros((tq, C_DV), F32) for _ in range(2)]
        if do_c:
            l1, l2 = row_sum[hc]
            r = lam * l1 * (1.0 / l2)
            acc = jnp.zeros((tq, C_DV), F32)
        for c in range(lk // kc):
            keys = slice(c * kc, (c + 1) * kc)
            if do_a:
                s_maps = lax.dot_general(q_maps, k_ref[keys, head_cols(ha)], dims, preferred_element_type=F32)
                for m in range(2):
                    s = s_maps[m * tq:(m + 1) * tq, :]
                    s_ref[ha % 3, m, :, keys] = s
                    part_max[m] = jnp.maximum(part_max[m], functools.reduce(jnp.maximum, lane_groups(s)))
            if do_b:
                for m in range(2):
                    e = jnp.exp2(s_ref[hb % 3, m, :, keys] - row_max[hb][m])
                    s_ref[hb % 3, m, :, keys] = e
                    part_sum[m] = part_sum[m] + functools.reduce(jnp.add, lane_groups(e))
            if do_c:
                a = (s_ref[hc % 3, 0, :, keys] - r * s_ref[hc % 3, 1, :, keys]).astype(BF16)
                acc = acc + _dot(a, v_ref[keys, head_cols(hc)])
        if do_a:
            row_max[ha] = [jnp.max(p, axis=-1, keepdims=True) for p in part_max]
        if do_b:
            row_sum[hb] = [jnp.sum(p, axis=-1, keepdims=True) for p in part_sum]
        if do_c:
            o = acc * (1.0 / l1)
            o = o * lax.rsqrt(jnp.mean(o * o, axis=-1, keepdims=True) + EPS) * sg_ref[...]
            o_ref[:, head_cols(hc)] = (o * (1.0 - lam_init)).astype(BF16)


def _attention(c_lambda, q, k, v, subln_g, lam_init, n_total, row0, prev_out):
    nb, lk, _ = k.shape
    lq = q.shape[0] // nb
    tq = _tile(lq, 256)
    nq = lq // tq
    rb = row0 // tq
    in_specs = [
        pl.BlockSpec((4, C_DK), lambda b, t: (0, 0)),
        pl.BlockSpec((tq, GROUP_W), lambda b, t: (b * nq + t, 0)),
        pl.BlockSpec((None, lk, GROUP_W), lambda b, t: (b, 0, 0)),
        pl.BlockSpec((None, lk, GROUP_W), lambda b, t: (b, 0, 0)),
        pl.BlockSpec((1, C_DV), lambda b, t: (0, 0)),
    ]
    args = [c_lambda, q, k, v, subln_g]
    aliases = {}
    if prev_out is not None:
        in_specs.append(pl.BlockSpec(memory_space=pl.ANY))
        args.append(prev_out)
        aliases = {5: 0}
    return pl.pallas_call(
        functools.partial(_attn_kernel, lam_init=lam_init),
        grid=(nb, nq),
        in_specs=in_specs,
        out_specs=pl.BlockSpec((tq, GROUP_W), lambda b, t: (rb + b * nq + t, 0)),
        out_shape=jax.ShapeDtypeStruct((n_total, GROUP_W), BF16),
        scratch_shapes=[pltpu.VMEM((3, 2, tq, lk), F32)],
        input_output_aliases=aliases,
        compiler_params=_params(("arbitrary", "arbitrary"), 56),
        name="diff_attn",
    )(*args)


def _dft_kernel(*refs, scale):
    dx_ref, wc_ref, c_ref, s_ref, dl_ref = refs[:5]
    o_ref, pq_ref = refs[-2], refs[-1]

    @pl.when(pl.program_id(1) == 0)
    def _():
        pq_ref[...] = _dot(dx_ref[...].astype(BF16), wc_ref[...]).astype(BF16)

    y = _dot(c_ref[...], pq_ref[:, 0:GROUP_W]) - _dot(s_ref[...], pq_ref[:, GROUP_W:2 * GROUP_W])
    o_ref[...] = _dot((y * scale).astype(BF16), dl_ref[...]).astype(BF16)


def _fourier(proj, wc, cl, sl, d_lin, nb, seq, n_total, row0, prev_out):
    tr = _tile(seq, 512)
    nr = seq // tr
    sb = row0 // seq
    rb = row0 // tr
    in_specs = [
        pl.BlockSpec((seq, GROUP_W), lambda b, r: (sb + b, N_GROUPS - 1)),
        pl.BlockSpec((GROUP_W, 2 * GROUP_W), lambda b, r: (0, 0)),
        pl.BlockSpec((tr, seq), lambda b, r: (r, 0)),
        pl.BlockSpec((tr, seq), lambda b, r: (r, 0)),
        pl.BlockSpec((GROUP_W, GROUP_W), lambda b, r: (0, 0)),
    ]
    args = [proj, wc, cl, sl, d_lin]
    aliases = {}
    if prev_out is not None:
        in_specs.append(pl.BlockSpec(memory_space=pl.ANY))
        args.append(prev_out)
        aliases = {5: 0}
    return pl.pallas_call(
        functools.partial(_dft_kernel, scale=1.0 / math.sqrt(seq * D_GW)),
        grid=(nb, nr),
        in_specs=in_specs,
        out_specs=pl.BlockSpec((tr, GROUP_W), lambda b, r: (rb + b * nr + r, 0)),
        out_shape=jax.ShapeDtypeStruct((n_total, GROUP_W), BF16),
        scratch_shapes=[pltpu.VMEM((seq, 2 * GROUP_W), BF16)],
        input_output_aliases=aliases,
        compiler_params=_params(("arbitrary", "arbitrary"), 56),
        name="fourier_mix",
    )(*args)


def _outproj_kernel(x_ref, mod_ref, a_ref, b_ref, c_ref, d_ref, w_ref, o_ref):
    cat = jnp.concatenate([a_ref[...], b_ref[...], c_ref[...], d_ref[...]], axis=1)
    o_ref[...] = x_ref[...] + mod_ref[5:6, :] * _dot(cat, w_ref[...])


def _outproj(x, modl, oa, ob, oc, od, w_out, layer, cond_of):
    n, d = x.shape
    tm = cond_of.tile(512)
    mix = pl.BlockSpec((tm, GROUP_W), lambda i: (i, 0))
    return pl.pallas_call(
        _outproj_kernel,
        grid=(n // tm,),
        in_specs=[
            pl.BlockSpec((tm, d), lambda i: (i, 0)),
            pl.BlockSpec((None, N_MOD, d), lambda i: (cond_of(i, tm), 0, 0)),
            mix, mix, mix, mix,
            pl.BlockSpec((None, 4 * GROUP_W, d), lambda i: (layer, 0, 0)),
        ],
        out_specs=pl.BlockSpec((tm, d), lambda i: (i, 0)),
        out_shape=jax.ShapeDtypeStruct((n, d), F32),
        compiler_params=_params(("arbitrary",), 48),
        name="out_proj",
    )(x, modl, oa, ob, oc, od, w_out)


def _dft_tables(n, split=128):
    def trig(cols, stride):
        j = lax.broadcasted_iota(jnp.int32, (n, cols), 0)
        k = lax.broadcasted_iota(jnp.int32, (n, cols), 1) * stride
        ang = (2.0 * math.pi / n) * lax.rem(j * k, n).astype(F32)
        return jnp.cos(ang), jnp.sin(ang)

    if n % split or n <= split:
        return trig(n, 1)
    (ch, sh), (cl, sl) = trig(n // split, split), trig(split, 1)
    ch, sh, cl, sl = ch[:, :, None], sh[:, :, None], cl[:, None, :], sl[:, None, :]
    return (ch * cl - sh * sl).reshape(n, n), (sh * cl + ch * sl).reshape(n, n)


def _channel_dft_table():
    c, s = _dft_tables(D_GW)
    eye = jnp.eye(GROUP_W // D_GW, dtype=F32)
    return jnp.concatenate([jnp.kron(eye, c), jnp.kron(eye, s)], axis=1).astype(BF16)


def _rope_tables(n_tokens):
    rows = n_tokens // GRID_W
    row = jnp.repeat(jnp.arange(rows, dtype=F32), GRID_W)
    col = jnp.tile(jnp.arange(GRID_W, dtype=F32), rows)
    freqs = ROPE_THETA ** (-jnp.arange(ROPE_NF, dtype=F32) / ROPE_NF)
    ang_r = row[:, None] * freqs
    ang_c = col[:, None] * freqs

    def lanes(fr, fc, sign):
        grp = jnp.concatenate([sign * fr, fr, sign * fc, fc], axis=-1)
        return jnp.tile(grp, (1, GROUP_W // C_DK))

    cos = lanes(jnp.cos(ang_r), jnp.cos(ang_c), 1.0)
    sin = lanes(jnp.sin(ang_r), jnp.sin(ang_c), -1.0)
    return cos, sin


class _CondOf:
    def __init__(self, n_prompt, seq_s):
        self.n_prompt = n_prompt
        self.seq_s = seq_s

    def tile(self, target):
        return _tile(math.gcd(self.n_prompt, self.seq_s), target)

    def __call__(self, i, tm):
        r = i * tm
        return jnp.where(r < self.n_prompt, 0, 1 + (r - self.n_prompt) // self.seq_s)


def kernel(x_prompt, x_sample, c, cache_k, cache_v, c_ctx, w_mod, b_mod, norm_g, w_ff1_in, w_ff1_down, w_ff2_in, w_ff2_down, w_in, w_out, a_norm_g, a_ws, a_bs, b_conv_w, b_conv_b, b_ln_g, b_ln_b, b_pw, c_qnorm_g, c_knorm_g, c_lambda, c_subln_g, d_lin):
    nbp, seq_p, d = x_prompt.shape
    nbs, seq_s, _ = x_sample.shape
    depth = w_mod.shape[0]
    past = cache_k.shape[2]
    n_p, n_s = nbp * seq_p, nbs * seq_s
    n = n_p + n_s
    assert n_p % seq_s == 0 and seq_p % CHUNK == 0 and seq_s % CHUNK == 0
    assert 1 + nbs <= COND_PAD
    cond_of = _CondOf(n_p, seq_s)

    cond = jnp.zeros((COND_PAD, d), F32).at[0].set(c_ctx).at[1:1 + nbs].set(c)
    mods = _modulation(cond, w_mod, b_mod).reshape(depth, COND_PAD, N_MOD, d)

    wc = _channel_dft_table()
    tabs_p = [t.astype(BF16) for t in _dft_tables(seq_p)]
    tabs_s = [t.astype(BF16) for t in _dft_tables(seq_s)]
    rope = _rope_tables(seq_s)
    gidx = jnp.arange(GROUP_W) // C_DK
    gm = jnp.where(gidx[:, None] == gidx[None, :], 1.0 / C_DK, 0.0).astype(BF16)

    assert w_ff1_down.shape[1] % FFN_TF == 0
    wu1, wd1 = w_ff1_in, w_ff1_down
    wu2, wd2 = w_ff2_in, w_ff2_down
    w_out_b = w_out.astype(BF16)

    new_kv = None
    for l in range(depth):
        lam_init = 0.8 - 0.6 * math.exp(-0.3 * l)
        modl = mods[l]
        ng = norm_g[l]
        ff1 = (modl, ng[0:1], wu1, wd1, l, 0, cond_of)
        if l == 0:
            x = _ffn(x_prompt.reshape(n_p, d), *ff1, out_total=n)
            x = _ffn(x_sample.reshape(n_s, d), *ff1, tok_row0=n_p, out_total=n, out_row0=n_p, prev_out=x)
        else:
            x = _ffn(x, *ff1)

        proj = _inproj(x, modl, ng[1:2], w_in, l, cond_of)

        bias_a = jnp.repeat(a_bs[l].T, CHUNK, axis=1)
        out_a = _mix_a(proj, a_norm_g[l][None], a_ws[l].astype(BF16), bias_a)

        out_b = _mix_b(proj, b_conv_w[l], b_conv_b[l][None], b_ln_g[l][None], b_ln_b[l][None],
                       b_pw[l].astype(BF16), n_p, seq_p, seq_s)

        gq = jnp.tile(c_qnorm_g[l], GROUP_W // C_DK)[None]
        gk = jnp.tile(c_knorm_g[l], GROUP_W // C_DK)[None]
        sg = c_subln_g[l][None]
        qp, kp, vp, *new_kv = _prep(proj, gq, gk, gm, 0, n_p, None, cache_out=(l, depth, seq_p, new_kv))
        qs, k_all, v_all = _prep(proj, gq, gk, gm, n_p, n_s, rope, past)
        k_all, v_all = _cache_fill(cache_k, cache_v, l, k_all, v_all)
        out_c = _attention(c_lambda[l], qp, kp.reshape(nbp, seq_p, GROUP_W), vp.reshape(nbp, seq_p, GROUP_W),
                           sg, lam_init, n, 0, None)
        out_c = _attention(c_lambda[l], qs, k_all, v_all, sg, lam_init, n, n_p, out_c)

        dl = d_lin[l].astype(BF16)
        out_d = _fourier(proj, wc, tabs_p[0], tabs_p[1], dl, nbp, seq_p, n, 0, None)
        out_d = _fourier(proj, wc, tabs_s[0], tabs_s[1], dl, nbs, seq_s, n, n_p, out_d)

        x = _outproj(x, modl, out_a, out_b, out_c, out_d, w_out_b, l, cond_of)
        ff2 = (modl, ng[2:3], wu2, wd2, l, 6, cond_of)
        if l == depth - 1:
            y_p = _ffn(x, *ff2, nrows=n_p)
            y_s = _ffn(x, *ff2, in_row0=n_p, nrows=n_s, tok_row0=n_p)
        else:
            x = _ffn(x, *ff2)

    new_k, new_v = new_kv
    return (y_p.reshape(nbp, seq_p, d), y_s.reshape(nbs, seq_s, d), new_k, new_v)
```
